```python
import functools
import jax, jax.numpy as jnp
from jax import lax
import numpy as np

D_MODEL = 1024
BATCH = 2
SEQ = 8192
DEPTH = 1
DEC_BATCH = 32
DEC_SEQ = 1
PAST_LEN = 16384
PAGE_SIZE = 128

HEAD_DIM = 64
N_HEADS_SB = 8
N_HEADS_FOX = 8
D_SB = N_HEADS_SB * HEAD_DIM
D_FOX = N_HEADS_FOX * HEAD_DIM
D_FF = 2816
Q_BLOCK = 128
N_SUBLAYERS = 3
EPS = 1e-6
FORGET_BIAS_INIT = 3.0
NEG_BIG = -1e30
D_IN = 3 * D_SB + 3 * D_FOX + N_HEADS_FOX + 2 * D_MODEL

kernel_name = "sb_fox_parallel_macaron_decoder_step"


def rmsnorm(x, g):
    xf = x.astype(jnp.float32)
    y = xf * lax.rsqrt(jnp.mean(xf * xf, axis=-1, keepdims=True) + EPS)
    return (y * g.astype(jnp.float32)).astype(x.dtype)


def adaln_params(c, w_ada, b_ada):
    mod = jax.nn.silu(c) @ w_ada + b_ada
    return jnp.split(mod[:, None, :], 3 * N_SUBLAYERS, axis=-1)


def modulate(x, shift, scale):
    return x * (1.0 + scale) + shift


def swiglu(x, w_up, w_down):
    g, u = jnp.split(x @ w_up, 2, axis=-1)
    return (jax.nn.silu(g) * u) @ w_down


def project_heads(n, w_in, b_forget):
    N, L, _ = n.shape
    p = n @ w_in
    bounds = [D_SB, 2 * D_SB, 3 * D_SB,
              3 * D_SB + D_FOX, 3 * D_SB + 2 * D_FOX, 3 * D_SB + 3 * D_FOX,
              3 * D_SB + 3 * D_FOX + N_HEADS_FOX,
              3 * D_SB + 3 * D_FOX + N_HEADS_FOX + D_MODEL]
    q_sb, k_sb, v_sb, q_fx, k_fx, v_fx, f_fx, g_sb, g_fx = jnp.split(p, bounds, axis=-1)
    sb = lambda t: t.reshape(N, L, N_HEADS_SB, HEAD_DIM)
    fx = lambda t: t.reshape(N, L, N_HEADS_FOX, HEAD_DIM)
    log_f = jax.nn.log_sigmoid((f_fx + b_forget).astype(jnp.float32))
    return sb(q_sb), sb(k_sb), sb(v_sb), fx(q_fx), fx(k_fx), fx(v_fx), log_f, g_sb, g_fx


def stick_breaking_weights(z, mask):
    log_1m = jnp.where(mask, jax.nn.log_sigmoid(-z), 0.0)
    rest = lax.cumsum(log_1m, axis=z.ndim - 1, reverse=True) - log_1m
    return jnp.where(mask, jnp.exp(jax.nn.log_sigmoid(z) + rest), 0.0)


def fox_weights(z, cum_q, cum_k, mask):
    logits = z + cum_q[..., :, None] - cum_k[..., None, :]
    logits = jnp.where(mask, logits, NEG_BIG)
    return jax.nn.softmax(logits, axis=-1)


def prompt_mixer(q_sb, k_sb, v_sb, q_fx, k_fx, v_fx, log_f):
    N, S = q_sb.shape[:2]
    scale = HEAD_DIM ** -0.5
    cum = jnp.cumsum(log_f, axis=1).transpose(0, 2, 1)
    kpos = jnp.arange(S)

    def block(i):
        start = i * Q_BLOCK
        qpos = start + jnp.arange(Q_BLOCK)
        qs = lax.dynamic_slice_in_dim(q_sb, start, Q_BLOCK, axis=1)
        z = jnp.einsum('bqhd,bkhd->bhqk', qs, k_sb).astype(jnp.float32) * scale
        a = stick_breaking_weights(z, kpos[None, :] < qpos[:, None])
        o_sb = jnp.einsum('bhqk,bkhd->bqhd', a.astype(v_sb.dtype), v_sb)
        qf = lax.dynamic_slice_in_dim(q_fx, start, Q_BLOCK, axis=1)
        zf = jnp.einsum('bqhd,bkhd->bhqk', qf, k_fx).astype(jnp.float32) * scale
        cq = lax.dynamic_slice_in_dim(cum, start, Q_BLOCK, axis=2)
        w = fox_weights(zf, cq, cum, kpos[None, :] <= qpos[:, None])
        o_fx = jnp.einsum('bhqk,bkhd->bqhd', w.astype(v_fx.dtype), v_fx)
        return o_sb, o_fx

    o_sb, o_fx = lax.map(block, jnp.arange(S // Q_BLOCK))
    o_sb = o_sb.transpose(1, 0, 2, 3, 4).reshape(N, S, D_SB)
    o_fx = o_fx.transpose(1, 0, 2, 3, 4).reshape(N, S, D_FOX)
    return o_sb, o_fx


def sample_mixer(q_sb, k_sb, v_sb, q_fx, k_fx, v_fx, log_f,
                 cache_sb_k, cache_sb_v, cache_fox_k, cache_fox_v, cache_fox_logf, page_table):
    N, T = q_sb.shape[:2]
    scale = HEAD_DIM ** -0.5

    def gather(cache):
        g = cache[page_table]
        return g.reshape((N, -1) + cache.shape[2:])

    kp_sb, vp_sb = gather(cache_sb_k), gather(cache_sb_v)
    kp_fx, vp_fx = gather(cache_fox_k), gather(cache_fox_v)
    lf_past = gather(cache_fox_logf).astype(jnp.float32)
    P = kp_sb.shape[1]
    qpos = P + jnp.arange(T)
    kpos = jnp.arange(P + T)

    z = jnp.concatenate([jnp.einsum('bqhd,bkhd->bhqk', q_sb, kp_sb),
                         jnp.einsum('bqhd,bkhd->bhqk', q_sb, k_sb)], axis=-1).astype(jnp.float32) * scale
    a = stick_breaking_weights(z, kpos[None, :] < qpos[:, None]).astype(v_sb.dtype)
    o_sb = (jnp.einsum('bhqk,bkhd->bqhd', a[..., :P], vp_sb)
            + jnp.einsum('bhqk,bkhd->bqhd', a[..., P:], v_sb))

    cum_past = jnp.cumsum(lf_past, axis=1)
    cum_new = cum_past[:, -1:] + jnp.cumsum(log_f, axis=1)
    cum_k = jnp.concatenate([cum_past, cum_new], axis=1).transpose(0, 2, 1)
    cum_q = cum_new.transpose(0, 2, 1)
    zf = jnp.concatenate([jnp.einsum('bqhd,bkhd->bhqk', q_fx, kp_fx),
                          jnp.einsum('bqhd,bkhd->bhqk', q_fx, k_fx)], axis=-1).astype(jnp.float32) * scale
    w = fox_weights(zf, cum_q, cum_k, kpos[None, :] <= qpos[:, None]).astype(v_fx.dtype)
    o_fx = (jnp.einsum('bhqk,bkhd->bqhd', w[..., :P], vp_fx)
            + jnp.einsum('bhqk,bkhd->bqhd', w[..., P:], v_fx))
    return o_sb.reshape(N, T, D_SB), o_fx.reshape(N, T, D_FOX)


def decoder_layer(x, c, mixer, norm_ffn1, norm_mix, norm_ffn2, w_ada, b_ada,
                  w_up1, w_down1, w_up2, w_down2, w_in, b_forget, w_br_sb, w_br_fox, w_out):
    sh1, sc1, gt1, sh2, sc2, gt2, sh3, sc3, gt3 = adaln_params(c, w_ada, b_ada)
    h = x + 0.5 * gt1 * swiglu(modulate(rmsnorm(x, norm_ffn1), sh1, sc1), w_up1, w_down1)
    n = modulate(rmsnorm(h, norm_mix), sh2, sc2)
    q_sb, k_sb, v_sb, q_fx, k_fx, v_fx, log_f, g_sb, g_fx = project_heads(n, w_in, b_forget)
    o_sb, o_fx = mixer(q_sb, k_sb, v_sb, q_fx, k_fx, v_fx, log_f)
    merged = jax.nn.sigmoid(g_sb) * (o_sb @ w_br_sb) + jax.nn.sigmoid(g_fx) * (o_fx @ w_br_fox)
    h = h + gt2 * (merged @ w_out)
    h = h + 0.5 * gt3 * swiglu(modulate(rmsnorm(h, norm_ffn2), sh3, sc3), w_up2, w_down2)
    return h, (k_sb, v_sb, k_fx, v_fx, log_f)


def setup_inputs(seed: int = 0) -> dict:
    key = jax.random.key(seed)
    ks = jax.random.split(key, 32)
    f32 = jnp.float32
    n_pages = PAST_LEN // PAGE_SIZE
    n_used = DEC_BATCH * n_pages
    n_pool = n_used + (n_used + 3) // 4
    nrm = lambda k, shape, s: jax.random.normal(k, shape, f32) * s
    gain = lambda k: 1.0 + nrm(k, (DEPTH, D_MODEL), 0.05)
    page_table = jax.random.permutation(ks[9], n_pool)[:n_used].reshape(DEC_BATCH, n_pages).astype(jnp.int32)
    return {
        "x_prompt": nrm(ks[0], (BATCH, SEQ, D_MODEL), 1.0),
        "x_sample": nrm(ks[1], (DEC_BATCH, DEC_SEQ, D_MODEL), 1.0),
        "c_prompt": nrm(ks[2], (BATCH, D_MODEL), 1.0),
        "c_sample": nrm(ks[3], (DEC_BATCH, D_MODEL), 1.0),
        "cache_sb_k": nrm(ks[4], (DEPTH, n_pool, PAGE_SIZE, N_HEADS_SB, HEAD_DIM), 1.0),
        "cache_sb_v": nrm(ks[5], (DEPTH, n_pool, PAGE_SIZE, N_HEADS_SB, HEAD_DIM), 1.0),
        "cache_fox_k": nrm(ks[6], (DEPTH, n_pool, PAGE_SIZE, N_HEADS_FOX, HEAD_DIM), 1.0),
        "cache_fox_v": nrm(ks[7], (DEPTH, n_pool, PAGE_SIZE, N_HEADS_FOX, HEAD_DIM), 1.0),
        "cache_fox_logf": jax.nn.log_sigmoid(FORGET_BIAS_INIT + nrm(ks[8], (DEPTH, n_pool, PAGE_SIZE, N_HEADS_FOX), 1.0)),
        "page_table": page_table,
        "norm_ffn1": gain(ks[10]),
        "norm_mix": gain(ks[11]),
        "norm_ffn2": gain(ks[12]),
        "w_ada": nrm(ks[13], (DEPTH, D_MODEL, 3 * N_SUBLAYERS * D_MODEL), 0.5 * D_MODEL ** -0.5),
        "b_ada": nrm(ks[14], (DEPTH, 3 * N_SUBLAYERS * D_MODEL), 0.02),
        "w_up1": nrm(ks[15], (DEPTH, D_MODEL, 2 * D_FF), D_MODEL ** -0.5),
        "w_down1": nrm(ks[16], (DEPTH, D_FF, D_MODEL), D_FF ** -0.5),
        "w_up2": nrm(ks[17], (DEPTH, D_MODEL, 2 * D_FF), D_MODEL ** -0.5),
        "w_down2": nrm(ks[18], (DEPTH, D_FF, D_MODEL), D_FF ** -0.5),
        "w_in": nrm(ks[19], (DEPTH, D_MODEL, D_IN), D_MODEL ** -0.5),
        "b_forget": FORGET_BIAS_INIT + nrm(ks[20], (DEPTH, N_HEADS_FOX), 0.1),
        "w_br_sb": nrm(ks[21], (DEPTH, D_SB, D_MODEL), D_SB ** -0.5),
        "w_br_fox": nrm(ks[22], (DEPTH, D_FOX, D_MODEL), D_FOX ** -0.5),
        "w_out": nrm(ks[23], (DEPTH, D_MODEL, D_MODEL), D_MODEL ** -0.5),
        "norm_final": 1.0 + nrm(ks[24], (D_MODEL,), 0.05),
    }


def reference(x_prompt, x_sample, c_prompt, c_sample, cache_sb_k, cache_sb_v, cache_fox_k, cache_fox_v,
              cache_fox_logf, page_table, norm_ffn1, norm_mix, norm_ffn2, w_ada, b_ada, w_up1, w_down1,
              w_up2, w_down2, w_in, b_forget, w_br_sb, w_br_fox, w_out, norm_final):
    yp, ys = x_prompt, x_sample
    st_p, st_s = [], []
    for l in range(DEPTH):
        w = (norm_ffn1[l], norm_mix[l], norm_ffn2[l], w_ada[l], b_ada[l], w_up1[l], w_down1[l],
             w_up2[l], w_down2[l], w_in[l], b_forget[l], w_br_sb[l], w_br_fox[l], w_out[l])
        yp, sp = decoder_layer(yp, c_prompt, prompt_mixer, *w)
        mix_s = functools.partial(sample_mixer, cache_sb_k=cache_sb_k[l], cache_sb_v=cache_sb_v[l],
                                  cache_fox_k=cache_fox_k[l], cache_fox_v=cache_fox_v[l],
                                  cache_fox_logf=cache_fox_logf[l], page_table=page_table)
        ys, ss = decoder_layer(ys, c_sample, mix_s, *w)
        st_p.append(sp)
        st_s.append(ss)
    y_prompt = rmsnorm(yp, norm_final)
    y_sample = rmsnorm(ys, norm_final)
    stack = lambda states, i: jnp.stack([s[i] for s in states], axis=0)
    return (y_prompt, y_sample,
            stack(st_p, 0), stack(st_p, 1), stack(st_p, 2), stack(st_p, 3), stack(st_p, 4),
            stack(st_s, 0), stack(st_s, 1), stack(st_s, 2), stack(st_s, 3), stack(st_s, 4))
```

```python
import functools

import jax
import jax.numpy as jnp
from jax import lax
from jax.experimental import pallas as pl
from jax.experimental.pallas import tpu as pltpu

HEAD_DIM = 64
N_HEADS = 8
D_HEADS = N_HEADS * HEAD_DIM
N_MOD = 9
EPS = 1e-6
LOG2E = 1.4426950408889634
QK_SCALE = HEAD_DIM ** -0.5
NEG_BIG = -1e30

LANES = 128
FF_CHUNK = 256
ROW_TILE = 512
ATT_TILE = 256
PAGES_PER_STEP = 8
VMEM_LIMIT = 56 * 1024 * 1024

BF = jnp.bfloat16
F32 = jnp.float32


def _dot(a, b):
    return jnp.dot(a, b, preferred_element_type=F32)


def _dot_nt(a, b):
    return lax.dot_general(a, b, (((1,), (1,)), ((), ())), preferred_element_type=F32)


def _split2(x):
    hi = x.astype(BF)
    lo = (x - hi.astype(F32)).astype(BF)
    return hi, lo


def _split3(x):
    hi = x.astype(BF)
    r = x - hi.astype(F32)
    mid = r.astype(BF)
    lo = (r - mid.astype(F32)).astype(BF)
    return hi, mid, lo


def _sigmoid(x):
    return 1.0 / (1.0 + jnp.exp(-x))


def _rms(x, g):
    ms = jnp.mean(x * x, axis=-1, keepdims=True)
    return x * lax.rsqrt(ms + EPS) * g


def _norm_mod(x, g, shift, scale):
    return _rms(x, g) * (1.0 + scale) + shift


def _tri(n, fn):
    r = lax.broadcasted_iota(jnp.int32, (n, n), 0)
    c = lax.broadcasted_iota(jnp.int32, (n, n), 1)
    return jnp.where(fn(r, c), 1.0, 0.0).astype(BF)


def _const_spec(shape):
    nd = len(shape)
    return pl.BlockSpec(shape, lambda *_: (0,) * nd, pipeline_mode=pl.Buffered(1))


def _params(sem):
    return pltpu.CompilerParams(dimension_semantics=sem, vmem_limit_bytes=VMEM_LIMIT)


def _adaln_kernel(c_ref, w_ref, b_ref, o_ref):
    c = c_ref[...]
    s = (c * _sigmoid(c)).astype(BF)
    o_ref[...] = _dot(s, w_ref[...].astype(BF)) + b_ref[...]


def _adaln(c, w, b):
    n, d = c.shape
    cols = w.shape[1]
    blk = d
    return pl.pallas_call(
        _adaln_kernel,
        grid=(cols // blk,),
        in_specs=[pl.BlockSpec((n, d), lambda j: (0, 0)),
                  pl.BlockSpec((d, blk), lambda j: (0, j)),
                  pl.BlockSpec((1, blk), lambda j: (0, j))],
        out_specs=pl.BlockSpec((n, blk), lambda j: (0, j)),
        out_shape=jax.ShapeDtypeStruct((n, cols), F32),
        compiler_params=_params(("arbitrary",)),
        name="adaln",
    )(c, w, b)


def _swiglu_acc(xn, wg_ref, wu_ref, wd_ref, acc_ref):
    acc_ref[...] = jnp.zeros_like(acc_ref)

    def chunk(c, carry):
        g = _dot(xn, wg_ref[c])
        u = _dot(xn, wu_ref[c])
        a = (g * _sigmoid(g) * u).astype(BF)
        acc_ref[...] += _dot(a, wd_ref[c])
        return carry

    lax.fori_loop(0, wg_ref.shape[0], chunk, 0)


def _ffn_kernel(x_ref, g_ref, sh_ref, sc_ref, gt_ref, wg_ref, wu_ref, wd_ref, o_ref, acc_ref):
    x = x_ref[...]
    xn = _norm_mod(x, g_ref[...], sh_ref[0], sc_ref[0]).astype(BF)
    _swiglu_acc(xn, wg_ref, wu_ref, wd_ref, acc_ref)
    o_ref[...] = x + 0.5 * gt_ref[0] * acc_ref[...]


def _mod_spec(mod, tm, tiles_per_group, j):
    lm = mod.shape[1]
    d = mod.shape[2] // N_MOD
    if lm == 1:
        return pl.BlockSpec((1, 1, d), lambda i: (i // tiles_per_group, 0, j))
    return pl.BlockSpec((1, tm, d), lambda i: (0, i, j))


def _ffn(x, mod, gain, wg, wu, wd, tm, tiles_per_group, first_mod):
    m, d = x.shape
    row = pl.BlockSpec((tm, d), lambda i: (i, 0))
    return pl.pallas_call(
        _ffn_kernel,
        grid=(m // tm,),
        in_specs=[row, _const_spec((1, d)),
                  _mod_spec(mod, tm, tiles_per_group, first_mod),
                  _mod_spec(mod, tm, tiles_per_group, first_mod + 1),
                  _mod_spec(mod, tm, tiles_per_group, first_mod + 2),
                  _const_spec(wg.shape), _const_spec(wu.shape), _const_spec(wd.shape)],
        out_specs=row,
        out_shape=jax.ShapeDtypeStruct((m, d), F32),
        scratch_shapes=[pltpu.VMEM((tm, d), F32)],
        compiler_params=_params(("arbitrary",)),
        name="ffn1",
    )(x, gain, mod, mod, mod, wg, wu, wd)


def _log_sigmoid(x):
    return jnp.minimum(x, 0.0) - jnp.log1p(jnp.exp(-jnp.abs(x)))


def _proj_prompt_kernel(h_ref, g_ref, sh_ref, sc_ref, wq_ref, wkv_ref, wf_ref, bf_ref,
                        q_ref, ksb_ref, vsb_ref, kfx_ref, vfx_ref, kvb_ref, lf_ref, cum_ref, carry_ref,
                        *, tiles_per_batch):
    i = pl.program_id(0)
    tm = h_ref.shape[0]
    n = _norm_mod(h_ref[...], g_ref[...], sh_ref[0], sc_ref[0]).astype(BF)
    q_ref[0] = (_dot(n, wq_ref[...]) * (QK_SCALE * LOG2E)).astype(BF)
    kv = _dot_nt(wkv_ref[...], n)
    for s, ref in enumerate((ksb_ref, vsb_ref, kfx_ref, vfx_ref)):
        ref[0] = kv[s * D_HEADS:(s + 1) * D_HEADS, :]
    for g in range(kvb_ref.shape[1]):
        for j in range(tm // ATT_TILE):
            kvb_ref[0, g, j] = kv[g * LANES:(g + 1) * LANES, j * ATT_TILE:(j + 1) * ATT_TILE].astype(BF)
    lf = _log_sigmoid(_dot_nt(wf_ref[...], n)[0:N_HEADS, :] + bf_ref[...])
    lf_ref[0] = lf

    @pl.when(i % tiles_per_batch == 0)
    def _():
        carry_ref[...] = jnp.zeros_like(carry_ref)

    tri = _tri(LANES, lambda r, c: r <= c)
    carry = carry_ref[...]
    for c in range(tm // LANES):
        hi, mid, lo = _split3(lf[:, c * LANES:(c + 1) * LANES])
        cs = _dot(hi, tri) + _dot(mid, tri) + _dot(lo, tri) + carry
        cum_ref[0, :, c * LANES:(c + 1) * LANES] = cs
        carry = jnp.broadcast_to(cs[:, LANES - 1:LANES], carry.shape)
    carry_ref[...] = carry


def _proj_prompt(h, mod, gain, wq, wkvt, wft, bf, batch, seq):
    m, d = h.shape
    tm = ROW_TILE
    tpb = seq // tm
    n_att = seq // ATT_TILE
    n_grp = wkvt.shape[0] // LANES
    t_spec = lambda rows: pl.BlockSpec((1, rows, tm), lambda i: (i // tpb, 0, i % tpb))
    t_shape = lambda rows: jax.ShapeDtypeStruct((batch, rows, seq), F32)
    return pl.pallas_call(
        functools.partial(_proj_prompt_kernel, tiles_per_batch=tpb),
        grid=(m // tm,),
        in_specs=[pl.BlockSpec((tm, d), lambda i: (i, 0)), _const_spec((1, d)),
                  _mod_spec(mod, tm, tpb, 3), _mod_spec(mod, tm, tpb, 4),
                  _const_spec(wq.shape), _const_spec(wkvt.shape), _const_spec(wft.shape),
                  _const_spec(bf.shape)],
        out_specs=[pl.BlockSpec((1, tm, wq.shape[1]), lambda i: (i // tpb, i % tpb, 0)),
                   t_spec(D_HEADS), t_spec(D_HEADS), t_spec(D_HEADS), t_spec(D_HEADS),
                   pl.BlockSpec((1, n_grp, tm // ATT_TILE, LANES, ATT_TILE),
                                lambda i: (i // tpb, 0, i % tpb, 0, 0)),
                   t_spec(N_HEADS), t_spec(N_HEADS)],
        out_shape=[jax.ShapeDtypeStruct((batch, seq, wq.shape[1]), BF),
                   t_shape(D_HEADS), t_shape(D_HEADS), t_shape(D_HEADS), t_shape(D_HEADS),
                   jax.ShapeDtypeStruct((batch, n_grp, n_att, LANES, ATT_TILE), BF),
                   t_shape(N_HEADS), t_shape(N_HEADS)],
        scratch_shapes=[pltpu.VMEM((N_HEADS, LANES), F32)],
        compiler_params=_params(("arbitrary",)),
        name="proj_prompt",
    )(h, gain, mod, mod, wq, wkvt, wft, bf)


def _head_masks():
    lane = lax.broadcasted_iota(jnp.int32, (ATT_TILE, LANES), 1)
    return lane < HEAD_DIM


def _sb_prompt_kernel(q_ref, k_ref, v_ref, o_ref):
    qi = pl.program_id(2)
    t = ATT_TILE
    first = _head_masks()
    q = q_ref[0]
    tri = _tri(t, lambda r, c: r > c)
    row = lax.broadcasted_iota(jnp.int32, (t, t), 0)
    col = lax.broadcasted_iota(jnp.int32, (t, t), 1)
    causal = col < row

    def block(qh, kb, carry, acc, mask):
        z = _dot(qh, k_ref[0, 0, kb])
        sp = jnp.maximum(z, 0.0) + jnp.log2(1.0 + jnp.exp2(-jnp.abs(z)))
        if mask is not None:
            sp = jnp.where(mask, sp, 0.0)
        hi, lo = _split2(sp)
        rest = _dot(hi, tri) + _dot(lo, tri)
        a = jnp.exp2(z - sp - rest - carry)
        if mask is not None:
            a = jnp.where(mask, a, 0.0)
        acc = acc + _dot_nt(a.astype(BF), v_ref[0, 0, kb])
        carry = carry + jnp.sum(sp, axis=-1, keepdims=True)
        return carry, acc

    outs = []
    for hh in range(2):
        qh = jnp.where(first if hh == 0 else jnp.logical_not(first), q, jnp.zeros_like(q))
        carry, acc = block(qh, qi, jnp.zeros((t, 1), F32), jnp.zeros((t, LANES), F32), causal)

        def body(j, ca, qh=qh):
            return block(qh, qi - 1 - j, ca[0], ca[1], None)

        carry, acc = lax.fori_loop(0, qi, body, (carry, acc))
        outs.append(acc)
    o_ref[0] = jnp.where(first, outs[0], outs[1]).astype(BF)


def _fox_prompt_kernel(q_ref, k_ref, v_ref, c_ref, o_ref):
    qi = pl.program_id(2)
    t = ATT_TILE
    first = _head_masks()
    q = q_ref[0]
    row = lax.broadcasted_iota(jnp.int32, (t, t), 0)
    col = lax.broadcasted_iota(jnp.int32, (t, t), 1)
    causal = col <= row

    def block(qh, hh, kb, base, m, l, acc, mask):
        z = _dot(qh, k_ref[0, 0, kb])
        ck = (c_ref[0, hh, pl.ds(kb, 1), :] - base) * LOG2E
        s = z - ck
        if mask is not None:
            s = jnp.where(mask, s, NEG_BIG)
        m_new = jnp.maximum(m, jnp.max(s, axis=-1, keepdims=True))
        alpha = jnp.exp2(m - m_new)
        p = jnp.exp2(s - m_new)
        l = alpha * l + jnp.sum(p, axis=-1, keepdims=True)
        acc = alpha * acc + _dot_nt(p.astype(BF), v_ref[0, 0, kb])
        return m_new, l, acc

    outs = []
    for hh in range(2):
        qh = jnp.where(first if hh == 0 else jnp.logical_not(first), q, jnp.zeros_like(q))
        base = c_ref[0, hh, pl.ds(qi, 1), :][:, 0:1]
        m0 = jnp.full((t, 1), NEG_BIG, F32)
        m, l, acc = block(qh, hh, qi, base, m0, jnp.zeros((t, 1), F32), jnp.zeros((t, LANES), F32), causal)

        def body(j, st, qh=qh, hh=hh, base=base):
            return block(qh, hh, qi - 1 - j, base, st[0], st[1], st[2], None)

        m, l, acc = lax.fori_loop(0, qi, body, (m, l, acc))
        outs.append(acc / l)
    o_ref[0] = jnp.where(first, outs[0], outs[1]).astype(BF)


def _prompt_attention(q, kvb, cum, batch, seq):
    t = ATT_TILE
    n_att = seq // t
    n_pair = D_HEADS // LANES
    grid = (batch, n_pair, n_att)
    sem = ("arbitrary", "arbitrary", "arbitrary")

    def kv_spec(first_group):
        return pl.BlockSpec((1, 1, n_att, LANES, t), lambda b, hp, qi: (b, first_group + hp, 0, 0, 0))

    out_spec = pl.BlockSpec((1, t, LANES), lambda b, hp, qi: (b, qi, hp))
    out_shape = jax.ShapeDtypeStruct((batch, seq, D_HEADS), BF)
    o_sb = pl.pallas_call(
        _sb_prompt_kernel,
        grid=grid,
        in_specs=[pl.BlockSpec((1, t, LANES), lambda b, hp, qi: (b, qi, hp)),
                  kv_spec(0), kv_spec(n_pair)],
        out_specs=out_spec,
        out_shape=out_shape,
        compiler_params=_params(sem),
        name="sb_prompt",
    )(q, kvb, kvb)
    o_fx = pl.pallas_call(
        _fox_prompt_kernel,
        grid=grid,
        in_specs=[pl.BlockSpec((1, t, LANES), lambda b, hp, qi: (b, qi, n_pair + hp)),
                  kv_spec(2 * n_pair), kv_spec(3 * n_pair),
                  pl.BlockSpec((1, 2, n_att, t), lambda b, hp, qi: (b, hp, 0, 0))],
        out_specs=out_spec,
        out_shape=out_shape,
        compiler_params=_params(sem),
        name="fox_prompt",
    )(q, kvb, kvb, cum)
    return o_sb, o_fx


def _post_kernel(h_ref, osb_ref, ofx_ref, gmix_ref, sh2_ref, sc2_ref, gt2_ref, gffn_ref, sh3_ref, sc3_ref,
                 gt3_ref, gfin_ref, wgs_ref, wgf_ref, wbs_ref, wbf_ref, wo_ref, wg_ref, wu_ref, wd_ref,
                 y_ref, acc_ref):
    h = h_ref[...]
    n = _norm_mod(h, gmix_ref[...], sh2_ref[0], sc2_ref[0]).astype(BF)
    merged = (_sigmoid(_dot(n, wgs_ref[...])) * _dot(osb_ref[...], wbs_ref[...])
              + _sigmoid(_dot(n, wgf_ref[...])) * _dot(ofx_ref[...], wbf_ref[...]))
    h2 = h + gt2_ref[0] * _dot(merged.astype(BF), wo_ref[...])
    xn = _norm_mod(h2, gffn_ref[...], sh3_ref[0], sc3_ref[0]).astype(BF)
    _swiglu_acc(xn, wg_ref, wu_ref, wd_ref, acc_ref)
    h3 = h2 + 0.5 * gt3_ref[0] * acc_ref[...]
    y_ref[...] = _rms(h3, gfin_ref[...])


def _post(h, o_sb, o_fx, mod, gmix, gffn, gfin, wgs, wgf, wbs, wbf, wo, wg, wu, wd, tm, tiles_per_group):
    m, d = h.shape
    row = pl.BlockSpec((tm, d), lambda i: (i, 0))
    orow = pl.BlockSpec((tm, o_sb.shape[1]), lambda i: (i, 0))
    ms = lambda j: _mod_spec(mod, tm, tiles_per_group, j)
    return pl.pallas_call(
        _post_kernel,
        grid=(m // tm,),
        in_specs=[row, orow, orow, _const_spec((1, d)), ms(3), ms(4), ms(5), _const_spec((1, d)), ms(6), ms(7),
                  ms(8), _const_spec((1, d)),
                  _const_spec(wgs.shape), _const_spec(wgf.shape), _const_spec(wbs.shape), _const_spec(wbf.shape),
                  _const_spec(wo.shape), _const_spec(wg.shape), _const_spec(wu.shape), _const_spec(wd.shape)],
        out_specs=row,
        out_shape=jax.ShapeDtypeStruct((m, d), F32),
        scratch_shapes=[pltpu.VMEM((tm, d), F32)],
        compiler_params=_params(("arbitrary",)),
        name="post",
    )(h, o_sb, o_fx, gmix, mod, mod, mod, gffn, mod, mod, mod, gfin, wgs, wgf, wbs, wbf, wo, wg, wu, wd)


def _proj_sample_kernel(h_ref, g_ref, sh_ref, sc_ref, w_ref, wf_ref, bf_ref, p_ref, lf_ref):
    n = _norm_mod(h_ref[...], g_ref[...], sh_ref[0], sc_ref[0]).astype(BF)
    p_ref[...] = _dot(n, w_ref[...])
    lf_ref[...] = _log_sigmoid(_dot_nt(n, wf_ref[...])[:, 0:N_HEADS] + bf_ref[...])


def _proj_sample(h, mod, gain, w, wft, bf_row):
    m, d = h.shape
    return pl.pallas_call(
        _proj_sample_kernel,
        grid=(1,),
        in_specs=[_const_spec((m, d)), _const_spec((1, d)), _mod_spec(mod, m, 1, 3), _mod_spec(mod, m, 1, 4),
                  _const_spec(w.shape), _const_spec(wft.shape), _const_spec(bf_row.shape)],
        out_specs=[_const_spec((m, w.shape[1])), _const_spec((m, N_HEADS))],
        out_shape=[jax.ShapeDtypeStruct((m, w.shape[1]), F32), jax.ShapeDtypeStruct((m, N_HEADS), F32)],
        compiler_params=_params(("arbitrary",)),
        name="proj_sample",
    )(h, gain, mod, mod, w, wft, bf_row)


def _decode_kernel(pt_ref, qb_ref, *rest, fox, n_steps):
    if fox:
        (knew_ref, vnew_ref, lfnew_ref, k_hbm, v_hbm, lf_hbm, o_ref,
         kbuf, vbuf, lfbuf, zbuf, abuf, acc_ref, carry_ref, m_ref, l_ref, sem) = rest
    else:
        k_hbm, v_hbm, o_ref, kbuf, vbuf, zbuf, abuf, acc_ref, carry_ref, sem = rest
    b = pl.program_id(0)
    c = pl.program_id(1)
    step = b * n_steps + c
    total = pl.num_programs(0) * n_steps
    ch = PAGES_PER_STEP

    def copies(s, slot):
        sb = s // n_steps
        first_page = (n_steps - 1 - s % n_steps) * ch
        out = []
        for p in range(ch):
            page = pt_ref[sb, first_page + p]
            out.append(pltpu.make_async_copy(k_hbm.at[page], kbuf.at[slot, p], sem.at[0, slot]))
            out.append(pltpu.make_async_copy(v_hbm.at[page], vbuf.at[slot, p], sem.at[1, slot]))
            if fox:
                out.append(pltpu.make_async_copy(lf_hbm.at[page], lfbuf.at[slot, p], sem.at[2, slot]))
        return out

    slot = step % 2

    @pl.when(step == 0)
    def _():
        for cp in copies(step, slot):
            cp.start()

    @pl.when(step + 1 < total)
    def _():
        for cp in copies(step + 1, 1 - slot):
            cp.start()

    lane = lax.broadcasted_iota(jnp.int32, (N_HEADS, LANES), 1)
    sel_r = lax.broadcasted_iota(jnp.int32, (N_HEADS, 2 * N_HEADS * 8), 0)
    sel_c = lax.broadcasted_iota(jnp.int32, (N_HEADS, 2 * N_HEADS * 8), 1)
    sel = jnp.where((sel_c % (N_HEADS * 8)) // 8 == sel_r, 1.0, 0.0).astype(BF)

    def scores(page_ref):
        parts = []
        for h in range(N_HEADS):
            part = page_ref[h, 0:8, :] * qb_ref[0, h, 0:8, :]
            for g in range(1, HEAD_DIM // 8):
                part = part + page_ref[h, g * 8:(g + 1) * 8, :] * qb_ref[0, h, g * 8:(g + 1) * 8, :]
            parts.append(part)
        stack = jnp.concatenate(parts, axis=0)
        hi, lo = _split2(stack)
        return _dot(sel, jnp.concatenate([hi, lo], axis=0))

    @pl.when(c == 0)
    def _():
        if fox:
            m_ref[...] = scores(knew_ref.at[0]) * (QK_SCALE * LOG2E)
            l_ref[...] = jnp.ones_like(l_ref)
            carry_ref[...] = jnp.broadcast_to(lfnew_ref[0], carry_ref.shape)
            for h in range(N_HEADS):
                acc_ref[h] = jnp.where(lax.broadcasted_iota(jnp.int32, (HEAD_DIM, LANES), 1) == 0,
                                       vnew_ref[0, h], 0.0)
        else:
            carry_ref[...] = jnp.zeros_like(carry_ref)
            acc_ref[...] = jnp.zeros_like(acc_ref)

    for cp in copies(step, slot):
        cp.wait()

    for p in range(ch):
        zbuf[p] = scores(kbuf.at[slot, p])

    carry = carry_ref[...]
    tri = _tri(LANES, lambda r, cc: r > cc)
    if fox:
        tri3 = jnp.concatenate([tri, tri, tri], axis=0)
        logits = [None] * ch
        for p in reversed(range(ch)):
            lf = lfbuf[slot, p]
            hi, mid, lo = _split3(lf)
            suffix = _dot(jnp.concatenate([hi, mid, lo], axis=1), tri3)
            logits[p] = (zbuf[p] * QK_SCALE + suffix + carry) * LOG2E
            carry = carry + jnp.sum(lf, axis=-1, keepdims=True)
        mx = logits[0]
        for p in range(1, ch):
            mx = jnp.maximum(mx, logits[p])
        m_old = m_ref[...]
        m_new = jnp.maximum(m_old, jnp.max(mx, axis=-1, keepdims=True))
        alpha = jnp.exp2(m_old - m_new)
        lsum = jnp.zeros((N_HEADS, LANES), F32)
        for p in range(ch):
            a = jnp.exp2(logits[p] - m_new)
            abuf[p] = a
            lsum = lsum + a
        l_ref[...] = alpha * l_ref[...] + jnp.sum(lsum, axis=-1, keepdims=True)
        m_ref[...] = m_new
    else:
        tri2 = jnp.concatenate([tri, tri], axis=0)
        for p in reversed(range(ch)):
            z = zbuf[p] * (QK_SCALE * LOG2E)
            sp = jnp.maximum(z, 0.0) + jnp.log2(1.0 + jnp.exp2(-jnp.abs(z)))
            hi, lo = _split2(sp)
            rest_ = _dot(jnp.concatenate([hi, lo], axis=1), tri2)
            abuf[p] = jnp.exp2(z - sp - rest_ - carry)
            carry = carry + jnp.sum(sp, axis=-1, keepdims=True)
    carry_ref[...] = carry

    for h in range(N_HEADS):
        acc = acc_ref[h]
        if fox:
            acc = acc * alpha[h:h + 1, :]
        for p in range(ch):
            acc = acc + abuf[p, h:h + 1, :] * vbuf[slot, p, h]
        acc_ref[h] = acc

    @pl.when(c == n_steps - 1)
    def _():
        for h in range(N_HEADS):
            acc = acc_ref[h]
            if fox:
                acc = acc / l_ref[h:h + 1, :]
            o_ref[0, h] = jnp.sum(acc, axis=-1, keepdims=True)


def _decode(page_table, qb, k_cache, v_cache, fox_args=None):
    n, n_pages = page_table.shape
    fox = fox_args is not None
    ch = PAGES_PER_STEP
    n_steps = n_pages // ch
    page = k_cache.shape[1:]
    seq_spec = pl.BlockSpec((1,) + qb.shape[1:], lambda b, c, pt: (b, 0, 0, 0))
    any_spec = pl.BlockSpec(memory_space=pl.ANY)
    in_specs = [seq_spec]
    args = [qb]
    scratch = [pltpu.VMEM((2, ch) + page, F32), pltpu.VMEM((2, ch) + page, F32)]
    if fox:
        knew, vnew, lfnew, lf_cache = fox_args
        in_specs += [seq_spec, seq_spec, pl.BlockSpec((1, N_HEADS, 1), lambda b, c, pt: (b, 0, 0))]
        args += [knew, vnew, lfnew]
        scratch.append(pltpu.VMEM((2, ch, N_HEADS, LANES), F32))
    in_specs += [any_spec, any_spec]
    args += [k_cache, v_cache]
    if fox:
        in_specs.append(any_spec)
        args.append(lf_cache)
    scratch += [pltpu.VMEM((ch, N_HEADS, LANES), F32), pltpu.VMEM((ch, N_HEADS, LANES), F32),
                pltpu.VMEM((N_HEADS, HEAD_DIM, LANES), F32), pltpu.VMEM((N_HEADS, LANES), F32)]
    if fox:
        scratch += [pltpu.VMEM((N_HEADS, LANES), F32), pltpu.VMEM((N_HEADS, LANES), F32)]
    scratch.append(pltpu.SemaphoreType.DMA((3, 2)))
    return pl.pallas_call(
        functools.partial(_decode_kernel, fox=fox, n_steps=n_steps),
        grid_spec=pltpu.PrefetchScalarGridSpec(
            num_scalar_prefetch=1,
            grid=(n, n_steps),
            in_specs=in_specs,
            out_specs=pl.BlockSpec((1, N_HEADS, HEAD_DIM, 1), lambda b, c, pt: (b, 0, 0, 0)),
            scratch_shapes=scratch),
        out_shape=jax.ShapeDtypeStruct((n, N_HEADS, HEAD_DIM, 1), F32),
        compiler_params=_params(("arbitrary", "arbitrary")),
        name="fox_decode" if fox else "sb_decode",
    )(page_table, *args)


def _chunked_ffn_weights(w_up, w_down):
    d, two_ff = w_up.shape
    ff = two_ff // 2
    nc = ff // FF_CHUNK
    wg = w_up[:, :ff].reshape(d, nc, FF_CHUNK).transpose(1, 0, 2).astype(BF)
    wu = w_up[:, ff:].reshape(d, nc, FF_CHUNK).transpose(1, 0, 2).astype(BF)
    wd = w_down.reshape(nc, FF_CHUNK, d).astype(BF)
    return wg, wu, wd


def _lane_rep(x):
    n = x.shape[0]
    return jnp.broadcast_to(x.reshape(n, N_HEADS, HEAD_DIM, 1), (n, N_HEADS, HEAD_DIM, LANES))


def kernel(x_prompt, x_sample, c_prompt, c_sample, cache_sb_k, cache_sb_v, cache_fox_k, cache_fox_v, cache_fox_logf, page_table, norm_ffn1, norm_mix, norm_ffn2, w_ada, b_ada, w_up1, w_down1, w_up2, w_down2, w_in, b_forget, w_br_sb, w_br_fox, w_out, norm_final):
    batch, seq, d = x_prompt.shape
    n_dec = x_sample.shape[0]
    depth = w_ada.shape[0]
    assert depth == 1 and x_sample.shape[1] == 1
    assert seq % ROW_TILE == 0 and ROW_TILE % ATT_TILE == 0
    assert page_table.shape[1] % PAGES_PER_STEP == 0 and cache_sb_k.shape[2] == LANES

    wt = jnp.transpose(w_in[0])
    dh = D_HEADS
    wq = jnp.concatenate([wt[0:dh], wt[3 * dh:4 * dh]], axis=0).T.astype(BF)
    wkvt = jnp.concatenate([wt[dh:3 * dh], wt[4 * dh:6 * dh]], axis=0).astype(BF)
    wqkv_s = wt[0:6 * dh].T.astype(BF)
    wft = jnp.concatenate([wt[6 * dh:6 * dh + N_HEADS], jnp.zeros((N_HEADS, d), F32)], axis=0).astype(BF)
    wgs = wt[6 * dh + N_HEADS:6 * dh + N_HEADS + d].T.astype(BF)
    wgf = wt[6 * dh + N_HEADS + d:].T.astype(BF)
    wbs, wbf, wo = w_br_sb[0].astype(BF), w_br_fox[0].astype(BF), w_out[0].astype(BF)
    ffn1_w = _chunked_ffn_weights(w_up1[0], w_down1[0])
    ffn2_w = _chunked_ffn_weights(w_up2[0], w_down2[0])
    g1, gm, g2, gf = norm_ffn1, norm_mix, norm_ffn2, norm_final.reshape(1, d)
    bf_col = b_forget.reshape(N_HEADS, 1)
    bf_row = b_forget.reshape(1, N_HEADS)

    n_c = batch + n_dec
    n_pad = -n_c % 8
    c_all = jnp.concatenate([c_prompt, c_sample, jnp.zeros((n_pad, d), F32)], axis=0)
    mod = _adaln(c_all, w_ada[0], b_ada)
    mod_p = mod[:batch].reshape(batch, 1, N_MOD * d)
    mod_s = mod[batch:n_c].reshape(1, n_dec, N_MOD * d)

    tpb = seq // ROW_TILE
    xp = x_prompt.reshape(batch * seq, d)
    h = _ffn(xp, mod_p, g1, *ffn1_w, ROW_TILE, tpb, 0)
    q, ksb_t, vsb_t, kfx_t, vfx_t, kvb, lft, cum = _proj_prompt(h, mod_p, gm, wq, wkvt, wft, bf_col, batch, seq)
    cum = cum.reshape(batch, N_HEADS, seq // ATT_TILE, ATT_TILE)
    o_sb, o_fx = _prompt_attention(q, kvb, cum, batch, seq)
    y_prompt = _post(h, o_sb.reshape(batch * seq, dh), o_fx.reshape(batch * seq, dh), mod_p, gm, g2, gf,
                     wgs, wgf, wbs, wbf, wo, *ffn2_w, ROW_TILE, tpb).reshape(batch, seq, d)

    def state(t):
        return jnp.transpose(t.reshape(batch, N_HEADS, HEAD_DIM, seq), (0, 3, 1, 2))[None]

    lf_prompt = jnp.transpose(lft, (0, 2, 1))[None]

    xs = x_sample.reshape(n_dec, d)
    hs = _ffn(xs, mod_s, g1, *ffn1_w, n_dec, 1, 0)
    ps, lfs = _proj_sample(hs, mod_s, gm, wqkv_s, wft, bf_row)
    q_sb, k_sb, v_sb, q_fx, k_fx, v_fx = [ps[:, i * dh:(i + 1) * dh] for i in range(6)]
    view = lambda cache: jnp.transpose(cache[0], (0, 2, 3, 1))
    o_sb_s = _decode(page_table, _lane_rep(q_sb), view(cache_sb_k), view(cache_sb_v))
    lf_view = jnp.transpose(cache_fox_logf[0], (0, 2, 1))
    o_fx_s = _decode(page_table, _lane_rep(q_fx), view(cache_fox_k), view(cache_fox_v),
                     (_lane_rep(k_fx), _lane_rep(v_fx), lfs.reshape(n_dec, N_HEADS, 1), lf_view))
    y_sample = _post(hs, o_sb_s.reshape(n_dec, dh).astype(BF), o_fx_s.reshape(n_dec, dh).astype(BF), mod_s,
                     gm, g2, gf, wgs, wgf, wbs, wbf, wo, *ffn2_w, n_dec, 1).reshape(n_dec, 1, d)

    st = lambda x: x.reshape(1, n_dec, 1, N_HEADS, HEAD_DIM)
    return (y_prompt, y_sample, state(ksb_t), state(vsb_t), state(kfx_t), state(vfx_t), lf_prompt,
            st(k_sb), st(v_sb), st(k_fx), st(v_fx), lfs.reshape(1, n_dec, 1, N_HEADS))
```

```python
import functools

import jax
import jax.numpy as jnp
from jax import lax
from jax.experimental import pallas as pl
from jax.experimental.pallas import tpu as pltpu

HEAD_DIM = 64
N_HEADS = 8
D_HEADS = N_HEADS * HEAD_DIM
N_MOD = 9
EPS = 1e-6
LOG2E = 1.4426950408889634
QK_SCALE = HEAD_DIM ** -0.5
NEG_BIG = -1e30

LANES = 128
FF_CHUNK = 256
ROW_TILE = 512
ATT_TILE = 256
PAIRS_PER_STEP = 4
PAGES_PER_STEP = 16
VMEM_LIMIT = 56 * 1024 * 1024

BF = jnp.bfloat16
F32 = jnp.float32


def _dot(a, b):
    return jnp.dot(a, b, preferred_element_type=F32)


def _dot_nt(a, b):
    return lax.dot_general(a, b, (((1,), (1,)), ((), ())), preferred_element_type=F32)


def _split2(x):
    hi = x.astype(BF)
    lo = (x - hi.astype(F32)).astype(BF)
    return hi, lo


def _split3(x):
    hi = x.astype(BF)
    r = x - hi.astype(F32)
    mid = r.astype(BF)
    lo = (r - mid.astype(F32)).astype(BF)
    return hi, mid, lo


def _sigmoid(x):
    return 1.0 / (1.0 + jnp.exp(-x))


def _rms(x, g):
    ms = jnp.mean(x * x, axis=-1, keepdims=True)
    return x * lax.rsqrt(ms + EPS) * g


def _norm_mod(x, g, shift, scale):
    return _rms(x, g) * (1.0 + scale) + shift


def _tri(n, fn):
    r = lax.broadcasted_iota(jnp.int32, (n, n), 0)
    c = lax.broadcasted_iota(jnp.int32, (n, n), 1)
    return jnp.where(fn(r, c), 1.0, 0.0).astype(BF)


def _const_spec(shape):
    nd = len(shape)
    return pl.BlockSpec(shape, lambda *_: (0,) * nd, pipeline_mode=pl.Buffered(1))


def _params(sem):
    return pltpu.CompilerParams(dimension_semantics=sem, vmem_limit_bytes=VMEM_LIMIT)


def _adaln_kernel(c_ref, w_ref, b_ref, o_ref):
    c = c_ref[...]
    s = (c * _sigmoid(c)).astype(BF)
    o_ref[...] = _dot(s, w_ref[...].astype(BF)) + b_ref[...]


def _adaln(c, w, b):
    n, d = c.shape
    cols = w.shape[1]
    blk = d
    return pl.pallas_call(
        _adaln_kernel,
        grid=(cols // blk,),
        in_specs=[pl.BlockSpec((n, d), lambda j: (0, 0)),
                  pl.BlockSpec((d, blk), lambda j: (0, j)),
                  pl.BlockSpec((1, blk), lambda j: (0, j))],
        out_specs=pl.BlockSpec((n, blk), lambda j: (0, j)),
        out_shape=jax.ShapeDtypeStruct((n, cols), F32),
        compiler_params=_params(("arbitrary",)),
        name="adaln",
    )(c, w, b)


def _swiglu_acc(xn, wg_ref, wu_ref, wd_ref, acc_ref):
    acc_ref[...] = jnp.zeros_like(acc_ref)

    def chunk(c, carry):
        g = _dot(xn, wg_ref[c])
        u = _dot(xn, wu_ref[c])
        a = (g * _sigmoid(g) * u).astype(BF)
        acc_ref[...] += _dot(a, wd_ref[c])
        return carry

    lax.fori_loop(0, wg_ref.shape[0], chunk, 0)


def _ffn_kernel(x_ref, g_ref, sh_ref, sc_ref, gt_ref, wg_ref, wu_ref, wd_ref, o_ref, acc_ref):
    x = x_ref[...]
    xn = _norm_mod(x, g_ref[...], sh_ref[0], sc_ref[0]).astype(BF)
    _swiglu_acc(xn, wg_ref, wu_ref, wd_ref, acc_ref)
    o_ref[...] = x + 0.5 * gt_ref[0] * acc_ref[...]


def _mod_spec(mod, tm, tiles_per_group, j):
    lm = mod.shape[1]
    d = mod.shape[2] // N_MOD
    if lm == 1:
        return pl.BlockSpec((1, 1, d), lambda i: (i // tiles_per_group, 0, j))
    return pl.BlockSpec((1, tm, d), lambda i: (0, i, j))


def _ffn(x, mod, gain, wg, wu, wd, tm, tiles_per_group, first_mod):
    m, d = x.shape
    row = pl.BlockSpec((tm, d), lambda i: (i, 0))
    return pl.pallas_call(
        _ffn_kernel,
        grid=(m // tm,),
        in_specs=[row, _const_spec((1, d)),
                  _mod_spec(mod, tm, tiles_per_group, first_mod),
                  _mod_spec(mod, tm, tiles_per_group, first_mod + 1),
                  _mod_spec(mod, tm, tiles_per_group, first_mod + 2),
                  _const_spec(wg.shape), _const_spec(wu.shape), _const_spec(wd.shape)],
        out_specs=row,
        out_shape=jax.ShapeDtypeStruct((m, d), F32),
        scratch_shapes=[pltpu.VMEM((tm, d), F32)],
        compiler_params=_params(("arbitrary",)),
        name="ffn1",
    )(x, gain, mod, mod, mod, wg, wu, wd)


def _log_sigmoid(x):
    return jnp.minimum(x, 0.0) - jnp.log1p(jnp.exp(-jnp.abs(x)))


def _proj_prompt_kernel(h_ref, g_ref, sh_ref, sc_ref, wq_ref, wkv_ref, wf_ref, bf_ref,
                        q_ref, ksb_ref, vsb_ref, kfx_ref, vfx_ref, kvb_ref, lf_ref, cum_ref, carry_ref,
                        *, tiles_per_batch):
    i = pl.program_id(0)
    tm = h_ref.shape[0]
    n = _norm_mod(h_ref[...], g_ref[...], sh_ref[0], sc_ref[0]).astype(BF)
    q_ref[0] = (_dot(n, wq_ref[...]) * (QK_SCALE * LOG2E)).astype(BF)
    kv = _dot_nt(wkv_ref[...], n)
    for s, ref in enumerate((ksb_ref, vsb_ref, kfx_ref, vfx_ref)):
        ref[0] = kv[s * D_HEADS:(s + 1) * D_HEADS, :]
    for g in range(kvb_ref.shape[1]):
        for j in range(tm // ATT_TILE):
            kvb_ref[0, g, j] = kv[g * LANES:(g + 1) * LANES, j * ATT_TILE:(j + 1) * ATT_TILE].astype(BF)
    lf = _log_sigmoid(_dot_nt(wf_ref[...], n)[0:N_HEADS, :] + bf_ref[...])
    lf_ref[0] = lf

    @pl.when(i % tiles_per_batch == 0)
    def _():
        carry_ref[...] = jnp.zeros_like(carry_ref)

    tri = _tri(LANES, lambda r, c: r <= c)
    carry = carry_ref[...]
    for c in range(tm // LANES):
        hi, mid, lo = _split3(lf[:, c * LANES:(c + 1) * LANES])
        cs = _dot(hi, tri) + _dot(mid, tri) + _dot(lo, tri) + carry
        cum_ref[0, :, c * LANES:(c + 1) * LANES] = cs
        carry = jnp.broadcast_to(cs[:, LANES - 1:LANES], carry.shape)
    carry_ref[...] = carry


def _proj_prompt(h, mod, gain, wq, wkvt, wft, bf, batch, seq):
    m, d = h.shape
    tm = ROW_TILE
    tpb = seq // tm
    n_att = seq // ATT_TILE
    n_grp = wkvt.shape[0] // LANES
    t_spec = lambda rows: pl.BlockSpec((1, rows, tm), lambda i: (i // tpb, 0, i % tpb))
    t_shape = lambda rows: jax.ShapeDtypeStruct((batch, rows, seq), F32)
    return pl.pallas_call(
        functools.partial(_proj_prompt_kernel, tiles_per_batch=tpb),
        grid=(m // tm,),
        in_specs=[pl.BlockSpec((tm, d), lambda i: (i, 0)), _const_spec((1, d)),
                  _mod_spec(mod, tm, tpb, 3), _mod_spec(mod, tm, tpb, 4),
                  _const_spec(wq.shape), _const_spec(wkvt.shape), _const_spec(wft.shape),
                  _const_spec(bf.shape)],
        out_specs=[pl.BlockSpec((1, tm, wq.shape[1]), lambda i: (i // tpb, i % tpb, 0)),
                   t_spec(D_HEADS), t_spec(D_HEADS), t_spec(D_HEADS), t_spec(D_HEADS),
                   pl.BlockSpec((1, n_grp, tm // ATT_TILE, LANES, ATT_TILE),
                                lambda i: (i // tpb, 0, i % tpb, 0, 0)),
                   t_spec(N_HEADS), t_spec(N_HEADS)],
        out_shape=[jax.ShapeDtypeStruct((batch, seq, wq.shape[1]), BF),
                   t_shape(D_HEADS), t_shape(D_HEADS), t_shape(D_HEADS), t_shape(D_HEADS),
                   jax.ShapeDtypeStruct((batch, n_grp, n_att, LANES, ATT_TILE), BF),
                   t_shape(N_HEADS), t_shape(N_HEADS)],
        scratch_shapes=[pltpu.VMEM((N_HEADS, LANES), F32)],
        compiler_params=_params(("arbitrary",)),
        name="proj_prompt",
    )(h, gain, mod, mod, wq, wkvt, wft, bf)


def _head_masks():
    lane = lax.broadcasted_iota(jnp.int32, (ATT_TILE, LANES), 1)
    return lane < HEAD_DIM


def _head_queries(q_ref):
    first = _head_masks()
    qs = []
    for pair in range(q_ref.shape[2] // LANES):
        q = q_ref[0, :, pair * LANES:(pair + 1) * LANES]
        zero = jnp.zeros_like(q)
        qs += [jnp.where(first, q, zero), jnp.where(first, zero, q)]
    return first, qs


def _skewed(n, stages):
    depth = len(stages)
    for step in range(n + depth - 1):
        for k in reversed(range(depth)):
            if 0 <= step - k < n:
                stages[k](step - k)


def _store_heads(o_ref, first, outs):
    for pair in range(len(outs) // 2):
        o_ref[0, :, pair * LANES:(pair + 1) * LANES] = jnp.where(first, outs[2 * pair], outs[2 * pair + 1]).astype(BF)


def _sb_prompt_kernel(q_ref, k_ref, v_ref, o_ref):
    qi = pl.program_id(2)
    t = ATT_TILE
    first, qs = _head_queries(q_ref)
    n_h = len(qs)
    tri = _tri(t, lambda r, c: r > c)
    tri2 = jnp.concatenate([tri, tri], axis=0)
    row = lax.broadcasted_iota(jnp.int32, (t, t), 0)
    col = lax.broadcasted_iota(jnp.int32, (t, t), 1)
    causal = col < row

    def blocks(kb, carries, accs, mask):
        heads = range(n_h)
        zs = [_dot(qs[h], k_ref[0, h // 2, kb]) for h in heads]
        sps = [jnp.maximum(z, 0.0) + jnp.log2(1.0 + jnp.exp2(-jnp.abs(z))) for z in zs]
        if mask is not None:
            sps = [jnp.where(mask, sp, 0.0) for sp in sps]
        splits = [_split2(sp) for sp in sps]
        rests = [_dot(jnp.concatenate([hi, lo], axis=1), tri2) for hi, lo in splits]
        probs = [jnp.exp2(zs[h] - sps[h] - rests[h] - carries[h]) for h in heads]
        if mask is not None:
            probs = [jnp.where(mask, a, 0.0) for a in probs]
        accs = [accs[h] + _dot_nt(probs[h].astype(BF), v_ref[0, h // 2, kb]) for h in heads]
        carries = [carries[h] + jnp.sum(sps[h], axis=-1, keepdims=True) for h in heads]
        return carries, accs

    carries, accs = blocks(qi, [jnp.zeros((t, 1), F32)] * n_h, [jnp.zeros((t, LANES), F32)] * n_h, causal)

    def body(j, st):
        carries, accs = blocks(qi - 1 - j, list(st[:n_h]), list(st[n_h:]), None)
        return tuple(carries) + tuple(accs)

    state = lax.fori_loop(0, qi, body, tuple(carries) + tuple(accs))
    _store_heads(o_ref, first, list(state[n_h:]))


def _fox_prompt_kernel(q_ref, k_ref, v_ref, c_ref, o_ref):
    qi = pl.program_id(2)
    t = ATT_TILE
    first, qs = _head_queries(q_ref)
    n_h = len(qs)
    row = lax.broadcasted_iota(jnp.int32, (t, t), 0)
    col = lax.broadcasted_iota(jnp.int32, (t, t), 1)
    causal = col <= row
    bases = [c_ref[0, h, pl.ds(qi, 1), :][:, 0:1] for h in range(n_h)]

    def blocks(kb, ms, ls, accs, mask):
        ms, ls, accs = list(ms), list(ls), list(accs)
        z, s, alpha, pv = {}, {}, {}, {}

        def logits(h):
            z[h] = _dot(qs[h], k_ref[0, h // 2, kb])

        def row_max(h):
            s[h] = z[h] - (c_ref[0, h, pl.ds(kb, 1), :] - bases[h]) * LOG2E
            if mask is not None:
                s[h] = jnp.where(mask, s[h], NEG_BIG)
            m_new = jnp.maximum(ms[h], jnp.max(s[h], axis=-1, keepdims=True))
            alpha[h] = jnp.exp2(ms[h] - m_new)
            ms[h] = m_new

        def weights(h):
            p = jnp.exp2(s[h] - ms[h])
            ls[h] = alpha[h] * ls[h] + jnp.sum(p, axis=-1, keepdims=True)
            pv[h] = _dot_nt(p.astype(BF), v_ref[0, h // 2, kb])

        def accumulate(h):
            accs[h] = alpha[h] * accs[h] + pv[h]

        _skewed(n_h, (logits, row_max, weights, accumulate))
        return ms, ls, accs

    ms, ls, accs = blocks(qi, [jnp.full((t, 1), NEG_BIG, F32)] * n_h, [jnp.zeros((t, 1), F32)] * n_h,
                          [jnp.zeros((t, LANES), F32)] * n_h, causal)

    def body(j, st):
        ms, ls, accs = blocks(qi - 1 - j, list(st[:n_h]), list(st[n_h:2 * n_h]), list(st[2 * n_h:]), None)
        return tuple(ms) + tuple(ls) + tuple(accs)

    state = lax.fori_loop(0, qi, body, tuple(ms) + tuple(ls) + tuple(accs))
    _store_heads(o_ref, first, [state[2 * n_h + h] / state[n_h + h] for h in range(n_h)])


def _prompt_attention(q, kvb, cum, batch, seq):
    t = ATT_TILE
    n_att = seq // t
    pps = PAIRS_PER_STEP
    n_grp = D_HEADS // LANES // pps
    grid = (batch, n_grp, n_att)
    sem = ("arbitrary", "arbitrary", "arbitrary")
    width = pps * LANES

    def kv_spec(first_block):
        return pl.BlockSpec((1, pps, n_att, LANES, t), lambda b, g, qi: (b, first_block + g, 0, 0, 0),
                            pipeline_mode=pl.Buffered(1))

    def q_spec(first_block):
        return pl.BlockSpec((1, t, width), lambda b, g, qi: (b, qi, first_block + g))

    out_spec = pl.BlockSpec((1, t, width), lambda b, g, qi: (b, qi, g))
    out_shape = jax.ShapeDtypeStruct((batch, seq, D_HEADS), BF)
    o_sb = pl.pallas_call(
        _sb_prompt_kernel,
        grid=grid,
        in_specs=[q_spec(0), kv_spec(0), kv_spec(n_grp)],
        out_specs=out_spec,
        out_shape=out_shape,
        compiler_params=_params(sem),
        name="sb_prompt",
    )(q, kvb, kvb)
    o_fx = pl.pallas_call(
        _fox_prompt_kernel,
        grid=grid,
        in_specs=[q_spec(n_grp), kv_spec(2 * n_grp), kv_spec(3 * n_grp),
                  pl.BlockSpec((1, 2 * pps, n_att, t), lambda b, g, qi: (b, g, 0, 0))],
        out_specs=out_spec,
        out_shape=out_shape,
        compiler_params=_params(sem),
        name="fox_prompt",
    )(q, kvb, kvb, cum)
    return o_sb, o_fx


def _post_kernel(h_ref, osb_ref, ofx_ref, gmix_ref, sh2_ref, sc2_ref, gt2_ref, gffn_ref, sh3_ref, sc3_ref,
                 gt3_ref, gfin_ref, wgs_ref, wgf_ref, wbs_ref, wbf_ref, wo_ref, wg_ref, wu_ref, wd_ref,
                 y_ref, acc_ref):
    h = h_ref[...]
    n = _norm_mod(h, gmix_ref[...], sh2_ref[0], sc2_ref[0]).astype(BF)
    merged = (_sigmoid(_dot(n, wgs_ref[...])) * _dot(osb_ref[...], wbs_ref[...])
              + _sigmoid(_dot(n, wgf_ref[...])) * _dot(ofx_ref[...], wbf_ref[...]))
    h2 = h + gt2_ref[0] * _dot(merged.astype(BF), wo_ref[...])
    xn = _norm_mod(h2, gffn_ref[...], sh3_ref[0], sc3_ref[0]).astype(BF)
    _swiglu_acc(xn, wg_ref, wu_ref, wd_ref, acc_ref)
    h3 = h2 + 0.5 * gt3_ref[0] * acc_ref[...]
    y_ref[...] = _rms(h3, gfin_ref[...])


def _post(h, o_sb, o_fx, mod, gmix, gffn, gfin, wgs, wgf, wbs, wbf, wo, wg, wu, wd, tm, tiles_per_group):
    m, d = h.shape
    row = pl.BlockSpec((tm, d), lambda i: (i, 0))
    orow = pl.BlockSpec((tm, o_sb.shape[1]), lambda i: (i, 0))
    ms = lambda j: _mod_spec(mod, tm, tiles_per_group, j)
    return pl.pallas_call(
        _post_kernel,
        grid=(m // tm,),
        in_specs=[row, orow, orow, _const_spec((1, d)), ms(3), ms(4), ms(5), _const_spec((1, d)), ms(6), ms(7),
                  ms(8), _const_spec((1, d)),
                  _const_spec(wgs.shape), _const_spec(wgf.shape), _const_spec(wbs.shape), _const_spec(wbf.shape),
                  _const_spec(wo.shape), _const_spec(wg.shape), _const_spec(wu.shape), _const_spec(wd.shape)],
        out_specs=row,
        out_shape=jax.ShapeDtypeStruct((m, d), F32),
        scratch_shapes=[pltpu.VMEM((tm, d), F32)],
        compiler_params=_params(("arbitrary",)),
        name="post",
    )(h, o_sb, o_fx, gmix, mod, mod, mod, gffn, mod, mod, mod, gfin, wgs, wgf, wbs, wbf, wo, wg, wu, wd)


def _proj_sample_kernel(h_ref, g_ref, sh_ref, sc_ref, w_ref, wf_ref, bf_ref, p_ref, lf_ref):
    n = _norm_mod(h_ref[...], g_ref[...], sh_ref[0], sc_ref[0]).astype(BF)
    p_ref[...] = _dot(n, w_ref[...])
    lf_ref[...] = _log_sigmoid(_dot_nt(n, wf_ref[...])[:, 0:N_HEADS] + bf_ref[...])


def _proj_sample(h, mod, gain, w, wft, bf_row):
    m, d = h.shape
    return pl.pallas_call(
        _proj_sample_kernel,
        grid=(1,),
        in_specs=[_const_spec((m, d)), _const_spec((1, d)), _mod_spec(mod, m, 1, 3), _mod_spec(mod, m, 1, 4),
                  _const_spec(w.shape), _const_spec(wft.shape), _const_spec(bf_row.shape)],
        out_specs=[_const_spec((m, w.shape[1])), _const_spec((m, N_HEADS))],
        out_shape=[jax.ShapeDtypeStruct((m, w.shape[1]), F32), jax.ShapeDtypeStruct((m, N_HEADS), F32)],
        compiler_params=_params(("arbitrary",)),
        name="proj_sample",
    )(h, gain, mod, mod, w, wft, bf_row)


def _decode_kernel(pt_ref, qb_ref, *rest, fox, n_steps):
    if fox:
        (knew_ref, vnew_ref, lfnew_ref, k_hbm, v_hbm, lf_hbm, o_ref,
         kbuf, vbuf, lfbuf, zbuf, abuf, acc_ref, carry_ref, m_ref, l_ref, sem) = rest
    else:
        k_hbm, v_hbm, o_ref, kbuf, vbuf, zbuf, abuf, acc_ref, carry_ref, sem = rest
    b = pl.program_id(0)
    c = pl.program_id(1)
    step = b * n_steps + c
    total = pl.num_programs(0) * n_steps
    ch = PAGES_PER_STEP

    def copies(s, slot):
        sb = s // n_steps
        first_page = (n_steps - 1 - s % n_steps) * ch
        out = []
        for p in range(ch):
            page = pt_ref[sb, first_page + p]
            out.append(pltpu.make_async_copy(k_hbm.at[page], kbuf.at[slot, p], sem.at[0, slot]))
            out.append(pltpu.make_async_copy(v_hbm.at[page], vbuf.at[slot, p], sem.at[1, slot]))
            if fox:
                out.append(pltpu.make_async_copy(lf_hbm.at[page], lfbuf.at[slot, p], sem.at[2, slot]))
        return out

    slot = step % 2

    @pl.when(step == 0)
    def _():
        for cp in copies(step, slot):
            cp.start()

    @pl.when(step + 1 < total)
    def _():
        for cp in copies(step + 1, 1 - slot):
            cp.start()

    lane = lax.broadcasted_iota(jnp.int32, (N_HEADS, LANES), 1)
    sel_r = lax.broadcasted_iota(jnp.int32, (N_HEADS, 2 * N_HEADS * 8), 0)
    sel_c = lax.broadcasted_iota(jnp.int32, (N_HEADS, 2 * N_HEADS * 8), 1)
    sel = jnp.where((sel_c % (N_HEADS * 8)) // 8 == sel_r, 1.0, 0.0).astype(BF)

    def scores(page_ref):
        parts = []
        for h in range(N_HEADS):
            part = page_ref[h, 0:8, :] * qb_ref[0, h, 0:8, :]
            for g in range(1, HEAD_DIM // 8):
                part = part + page_ref[h, g * 8:(g + 1) * 8, :] * qb_ref[0, h, g * 8:(g + 1) * 8, :]
            parts.append(part)
        stack = jnp.concatenate(parts, axis=0)
        hi, lo = _split2(stack)
        return _dot(sel, jnp.concatenate([hi, lo], axis=0))

    @pl.when(c == 0)
    def _():
        if fox:
            m_ref[...] = scores(knew_ref.at[0]) * (QK_SCALE * LOG2E)
            l_ref[...] = jnp.ones_like(l_ref)
            carry_ref[...] = jnp.broadcast_to(lfnew_ref[0], carry_ref.shape)
            for h in range(N_HEADS):
                acc_ref[h] = jnp.where(lax.broadcasted_iota(jnp.int32, (HEAD_DIM, LANES), 1) == 0,
                                       vnew_ref[0, h], 0.0)
        else:
            carry_ref[...] = jnp.zeros_like(carry_ref)
            acc_ref[...] = jnp.zeros_like(acc_ref)

    for cp in copies(step, slot):
        cp.wait()

    for p in range(ch):
        zbuf[p] = scores(kbuf.at[slot, p])

    carry = carry_ref[...]
    tri = _tri(LANES, lambda r, cc: r > cc)
    if fox:
        tri3 = jnp.concatenate([tri, tri, tri], axis=0)
        logits = [None] * ch
        for p in reversed(range(ch)):
            lf = lfbuf[slot, p]
            hi, mid, lo = _split3(lf)
            suffix = _dot(jnp.concatenate([hi, mid, lo], axis=1), tri3)
            logits[p] = (zbuf[p] * QK_SCALE + suffix + carry) * LOG2E
            carry = carry + jnp.sum(lf, axis=-1, keepdims=True)
        mx = logits[0]
        for p in range(1, ch):
            mx = jnp.maximum(mx, logits[p])
        m_old = m_ref[...]
        m_new = jnp.maximum(m_old, jnp.max(mx, axis=-1, keepdims=True))
        alpha = jnp.exp2(m_old - m_new)
        lsum = jnp.zeros((N_HEADS, LANES), F32)
        for p in range(ch):
            a = jnp.exp2(logits[p] - m_new)
            abuf[p] = a
            lsum = lsum + a
        l_ref[...] = alpha * l_ref[...] + jnp.sum(lsum, axis=-1, keepdims=True)
        m_ref[...] = m_new
    else:
        tri2 = jnp.concatenate([tri, tri], axis=0)
        for p in reversed(range(ch)):
            z = zbuf[p] * (QK_SCALE * LOG2E)
            sp = jnp.maximum(z, 0.0) + jnp.log2(1.0 + jnp.exp2(-jnp.abs(z)))
            hi, lo = _split2(sp)
            rest_ = _dot(jnp.concatenate([hi, lo], axis=1), tri2)
            abuf[p] = jnp.exp2(z - sp - rest_ - carry)
            carry = carry + jnp.sum(sp, axis=-1, keepdims=True)
    carry_ref[...] = carry

    for h in range(N_HEADS):
        acc = acc_ref[h]
        if fox:
            acc = acc * alpha[h:h + 1, :]
        for p in range(ch):
            acc = acc + abuf[p, h:h + 1, :] * vbuf[slot, p, h]
        acc_ref[h] = acc

    @pl.when(c == n_steps - 1)
    def _():
        for h in range(N_HEADS):
            acc = acc_ref[h]
            if fox:
                acc = acc / l_ref[h:h + 1, :]
            o_ref[0, h] = jnp.sum(acc, axis=-1, keepdims=True)


def _decode(page_table, qb, k_cache, v_cache, fox_args=None):
    n, n_pages = page_table.shape
    fox = fox_args is not None
    ch = PAGES_PER_STEP
    n_steps = n_pages // ch
    page = k_cache.shape[1:]
    seq_spec = pl.BlockSpec((1,) + qb.shape[1:], lambda b, c, pt: (b, 0, 0, 0))
    any_spec = pl.BlockSpec(memory_space=pl.ANY)
    in_specs = [seq_spec]
    args = [qb]
    scratch = [pltpu.VMEM((2, ch) + page, F32), pltpu.VMEM((2, ch) + page, F32)]
    if fox:
        knew, vnew, lfnew, lf_cache = fox_args
        in_specs += [seq_spec, seq_spec, pl.BlockSpec((1, N_HEADS, 1), lambda b, c, pt: (b, 0, 0))]
        args += [knew, vnew, lfnew]
        scratch.append(pltpu.VMEM((2, ch, N_HEADS, LANES), F32))
    in_specs += [any_spec, any_spec]
    args += [k_cache, v_cache]
    if fox:
        in_specs.append(any_spec)
        args.append(lf_cache)
    scratch += [pltpu.VMEM((ch, N_HEADS, LANES), F32), pltpu.VMEM((ch, N_HEADS, LANES), F32),
                pltpu.VMEM((N_HEADS, HEAD_DIM, LANES), F32), pltpu.VMEM((N_HEADS, LANES), F32)]
    if fox:
        scratch += [pltpu.VMEM((N_HEADS, LANES), F32), pltpu.VMEM((N_HEADS, LANES), F32)]
    scratch.append(pltpu.SemaphoreType.DMA((3, 2)))
    return pl.pallas_call(
        functools.partial(_decode_kernel, fox=fox, n_steps=n_steps),
        grid_spec=pltpu.PrefetchScalarGridSpec(
            num_scalar_prefetch=1,
            grid=(n, n_steps),
            in_specs=in_specs,
            out_specs=pl.BlockSpec((1, N_HEADS, HEAD_DIM, 1), lambda b, c, pt: (b, 0, 0, 0)),
            scratch_shapes=scratch),
        out_shape=jax.ShapeDtypeStruct((n, N_HEADS, HEAD_DIM, 1), F32),
        compiler_params=_params(("arbitrary", "arbitrary")),
        name="fox_decode" if fox else "sb_decode",
    )(page_table, *args)


def _chunked_ffn_weights(w_up, w_down):
    d, two_ff = w_up.shape
    ff = two_ff // 2
    nc = ff // FF_CHUNK
    wg = w_up[:, :ff].reshape(d, nc, FF_CHUNK).transpose(1, 0, 2).astype(BF)
    wu = w_up[:, ff:].reshape(d, nc, FF_CHUNK).transpose(1, 0, 2).astype(BF)
    wd = w_down.reshape(nc, FF_CHUNK, d).astype(BF)
    return wg, wu, wd


def _lane_rep(x):
    n = x.shape[0]
    return jnp.broadcast_to(x.reshape(n, N_HEADS, HEAD_DIM, 1), (n, N_HEADS, HEAD_DIM, LANES))


def kernel(x_prompt, x_sample, c_prompt, c_sample, cache_sb_k, cache_sb_v, cache_fox_k, cache_fox_v, cache_fox_logf, page_table, norm_ffn1, norm_mix, norm_ffn2, w_ada, b_ada, w_up1, w_down1, w_up2, w_down2, w_in, b_forget, w_br_sb, w_br_fox, w_out, norm_final):
    batch, seq, d = x_prompt.shape
    n_dec = x_sample.shape[0]
    depth = w_ada.shape[0]
    assert depth == 1 and x_sample.shape[1] == 1
    assert seq % ROW_TILE == 0 and ROW_TILE % ATT_TILE == 0
    assert page_table.shape[1] % PAGES_PER_STEP == 0 and cache_sb_k.shape[2] == LANES

    wt = jnp.transpose(w_in[0])
    dh = D_HEADS
    wq = jnp.concatenate([wt[0:dh], wt[3 * dh:4 * dh]], axis=0).T.astype(BF)
    wkvt = jnp.concatenate([wt[dh:3 * dh], wt[4 * dh:6 * dh]], axis=0).astype(BF)
    wqkv_s = wt[0:6 * dh].T.astype(BF)
    wft = jnp.concatenate([wt[6 * dh:6 * dh + N_HEADS], jnp.zeros((N_HEADS, d), F32)], axis=0).astype(BF)
    wgs = wt[6 * dh + N_HEADS:6 * dh + N_HEADS + d].T.astype(BF)
    wgf = wt[6 * dh + N_HEADS + d:].T.astype(BF)
    wbs, wbf, wo = w_br_sb[0].astype(BF), w_br_fox[0].astype(BF), w_out[0].astype(BF)
    ffn1_w = _chunked_ffn_weights(w_up1[0], w_down1[0])
    ffn2_w = _chunked_ffn_weights(w_up2[0], w_down2[0])
    g1, gm, g2, gf = norm_ffn1, norm_mix, norm_ffn2, norm_final.reshape(1, d)
    bf_col = b_forget.reshape(N_HEADS, 1)
    bf_row = b_forget.reshape(1, N_HEADS)

    n_c = batch + n_dec
    n_pad = -n_c % 8
    c_all = jnp.concatenate([c_prompt, c_sample, jnp.zeros((n_pad, d), F32)], axis=0)
    mod = _adaln(c_all, w_ada[0], b_ada)
    mod_p = mod[:batch].reshape(batch, 1, N_MOD * d)
    mod_s = mod[batch:n_c].reshape(1, n_dec, N_MOD * d)

    tpb = seq // ROW_TILE
    xp = x_prompt.reshape(batch * seq, d)
    h = _ffn(xp, mod_p, g1, *ffn1_w, ROW_TILE, tpb, 0)
    q, ksb_t, vsb_t, kfx_t, vfx_t, kvb, lft, cum = _proj_prompt(h, mod_p, gm, wq, wkvt, wft, bf_col, batch, seq)
    cum = cum.reshape(batch, N_HEADS, seq // ATT_TILE, ATT_TILE)
    o_sb, o_fx = _prompt_attention(q, kvb, cum, batch, seq)
    y_prompt = _post(h, o_sb.reshape(batch * seq, dh), o_fx.reshape(batch * seq, dh), mod_p, gm, g2, gf,
                     wgs, wgf, wbs, wbf, wo, *ffn2_w, ROW_TILE, tpb).reshape(batch, seq, d)

    def state(t):
        return jnp.transpose(t.reshape(batch, N_HEADS, HEAD_DIM, seq), (0, 3, 1, 2))[None]

    lf_prompt = jnp.transpose(lft, (0, 2, 1))[None]

    xs = x_sample.reshape(n_dec, d)
    hs = _ffn(xs, mod_s, g1, *ffn1_w, n_dec, 1, 0)
    ps, lfs = _proj_sample(hs, mod_s, gm, wqkv_s, wft, bf_row)
    q_sb, k_sb, v_sb, q_fx, k_fx, v_fx = [ps[:, i * dh:(i + 1) * dh] for i in range(6)]
    view = lambda cache: jnp.transpose(cache[0], (0, 2, 3, 1))
    o_sb_s = _decode(page_table, _lane_rep(q_sb), view(cache_sb_k), view(cache_sb_v))
    lf_view = jnp.transpose(cache_fox_logf[0], (0, 2, 1))
    o_fx_s = _decode(page_table, _lane_rep(q_fx), view(cache_fox_k), view(cache_fox_v),
                     (_lane_rep(k_fx), _lane_rep(v_fx), lfs.reshape(n_dec, N_HEADS, 1), lf_view))
    y_sample = _post(hs, o_sb_s.reshape(n_dec, dh).astype(BF), o_fx_s.reshape(n_dec, dh).astype(BF), mod_s,
                     gm, g2, gf, wgs, wgf, wbs, wbf, wo, *ffn2_w, n_dec, 1).reshape(n_dec, 1, d)

    st = lambda x: x.reshape(1, n_dec, 1, N_HEADS, HEAD_DIM)
    return (y_prompt, y_sample, state(ksb_t), state(vsb_t), state(kfx_t), state(vfx_t), lf_prompt,
            st(k_sb), st(v_sb), st(k_fx), st(v_fx), lfs.reshape(1, n_dec, 1, N_HEADS))
```

```python
import functools

import jax
import jax.numpy as jnp
from jax import lax
from jax.experimental import pallas as pl
from jax.experimental.pallas import tpu as pltpu

HEAD_DIM = 64
N_HEADS = 8
D_HEADS = N_HEADS * HEAD_DIM
N_MOD = 9
EPS = 1e-6
LOG2E = 1.4426950408889634
QK_SCALE = HEAD_DIM ** -0.5
NEG_BIG = -1e30

LANES = 128
FF_CHUNK = 256
ROW_TILE = 512
ATT_TILE = 256
FOX_KEY_TILE = 512
PAIRS_PER_STEP = 4
PAGES_PER_STEP = 16
VMEM_LIMIT = 56 * 1024 * 1024

BF = jnp.bfloat16
F32 = jnp.float32


def _dot(a, b):
    return jnp.dot(a, b, preferred_element_type=F32)


def _dot_nt(a, b):
    return lax.dot_general(a, b, (((1,), (1,)), ((), ())), preferred_element_type=F32)


def _split2(x):
    hi = x.astype(BF)
    lo = (x - hi.astype(F32)).astype(BF)
    return hi, lo


def _split3(x):
    hi = x.astype(BF)
    r = x - hi.astype(F32)
    mid = r.astype(BF)
    lo = (r - mid.astype(F32)).astype(BF)
    return hi, mid, lo


def _sigmoid(x):
    return 1.0 / (1.0 + jnp.exp(-x))


def _rms(x, g):
    ms = jnp.mean(x * x, axis=-1, keepdims=True)
    return x * lax.rsqrt(ms + EPS) * g


def _norm_mod(x, g, shift, scale):
    return _rms(x, g) * (1.0 + scale) + shift


def _tri(n, fn):
    r = lax.broadcasted_iota(jnp.int32, (n, n), 0)
    c = lax.broadcasted_iota(jnp.int32, (n, n), 1)
    return jnp.where(fn(r, c), 1.0, 0.0).astype(BF)


def _const_spec(shape):
    nd = len(shape)
    return pl.BlockSpec(shape, lambda *_: (0,) * nd, pipeline_mode=pl.Buffered(1))


def _params(sem):
    return pltpu.CompilerParams(dimension_semantics=sem, vmem_limit_bytes=VMEM_LIMIT)


def _adaln_kernel(c_ref, w_ref, b_ref, o_ref):
    c = c_ref[...]
    s = (c * _sigmoid(c)).astype(BF)
    o_ref[...] = _dot(s, w_ref[...].astype(BF)) + b_ref[...]


def _adaln(c, w, b):
    n, d = c.shape
    cols = w.shape[1]
    blk = d
    return pl.pallas_call(
        _adaln_kernel,
        grid=(cols // blk,),
        in_specs=[pl.BlockSpec((n, d), lambda j: (0, 0)),
                  pl.BlockSpec((d, blk), lambda j: (0, j)),
                  pl.BlockSpec((1, blk), lambda j: (0, j))],
        out_specs=pl.BlockSpec((n, blk), lambda j: (0, j)),
        out_shape=jax.ShapeDtypeStruct((n, cols), F32),
        compiler_params=_params(("arbitrary",)),
        name="adaln",
    )(c, w, b)


def _swiglu_acc(xn, wup_ref, wd_ref, acc_ref):
    ff = wd_ref.shape[0]
    for c in range(ff // FF_CHUNK):
        lo = c * FF_CHUNK
        g = _dot(xn, wup_ref[:, lo:lo + FF_CHUNK])
        u = _dot(xn, wup_ref[:, ff + lo:ff + lo + FF_CHUNK])
        a = (g * _sigmoid(g) * u).astype(BF)
        part = _dot(a, wd_ref[lo:lo + FF_CHUNK, :])
        if c == 0:
            acc_ref[...] = part
        else:
            acc_ref[...] += part


def _ffn_kernel(x_ref, g_ref, sh_ref, sc_ref, gt_ref, wup_ref, wd_ref, o_ref, acc_ref):
    x = x_ref[...]
    xn = _norm_mod(x, g_ref[...], sh_ref[0], sc_ref[0]).astype(BF)
    _swiglu_acc(xn, wup_ref, wd_ref, acc_ref)
    o_ref[...] = x + 0.5 * gt_ref[0] * acc_ref[...]


def _mod_spec(mod, tm, tiles_per_group, j):
    lm = mod.shape[1]
    d = mod.shape[2] // N_MOD
    if lm == 1:
        return pl.BlockSpec((1, 1, d), lambda i: (i // tiles_per_group, 0, j))
    return pl.BlockSpec((1, tm, d), lambda i: (0, i, j))


def _ffn(x, mod, gain, wup, wd, tm, tiles_per_group, first_mod):
    m, d = x.shape
    row = pl.BlockSpec((tm, d), lambda i: (i, 0))
    return pl.pallas_call(
        _ffn_kernel,
        grid=(m // tm,),
        in_specs=[row, _const_spec((1, d)),
                  _mod_spec(mod, tm, tiles_per_group, first_mod),
                  _mod_spec(mod, tm, tiles_per_group, first_mod + 1),
                  _mod_spec(mod, tm, tiles_per_group, first_mod + 2),
                  _const_spec(wup.shape), _const_spec(wd.shape)],
        out_specs=row,
        out_shape=jax.ShapeDtypeStruct((m, d), F32),
        scratch_shapes=[pltpu.VMEM((tm, d), F32)],
        compiler_params=_params(("arbitrary",)),
        name="ffn1",
    )(x, gain, mod, mod, mod, wup, wd)


def _log_sigmoid(x):
    return jnp.minimum(x, 0.0) - jnp.log1p(jnp.exp(-jnp.abs(x)))


def _proj_prompt_kernel(h_ref, g_ref, sh_ref, sc_ref, wq_ref, wkv_ref, wf_ref, bf_ref,
                        q_ref, ksb_ref, vsb_ref, kfx_ref, vfx_ref, sbb_ref, fxk_ref, fxv_ref, lf_ref, cum_ref,
                        carry_ref, *, tiles_per_batch):
    i = pl.program_id(0)
    tm = h_ref.shape[0]
    n = _norm_mod(h_ref[...], g_ref[...], sh_ref[0], sc_ref[0]).astype(BF)
    q_ref[0] = (_dot(n, wq_ref[...]) * (QK_SCALE * LOG2E)).astype(BF)
    kv = _dot_nt(wkv_ref[...], n)
    for s, ref in enumerate((ksb_ref, vsb_ref, kfx_ref, vfx_ref)):
        ref[0] = kv[s * D_HEADS:(s + 1) * D_HEADS, :]

    def pair_tiles(ref, first_row):
        n_grp, n_blk, _, width = ref.shape[1:]
        for g in range(n_grp):
            r0 = first_row + g * LANES
            for j in range(n_blk):
                ref[0, g, j] = kv[r0:r0 + LANES, j * width:(j + 1) * width].astype(BF)

    pair_tiles(sbb_ref, 0)
    pair_tiles(fxk_ref, 2 * D_HEADS)
    n_blk, _, width = fxv_ref.shape[2:]
    one_row = jnp.where(lax.broadcasted_iota(jnp.int32, (8, width), 0) == 0, 1.0, 0.0).astype(BF)
    other = jnp.concatenate([one_row, jnp.zeros((HEAD_DIM - 8, width), BF)], axis=0)
    for h in range(N_HEADS):
        r0 = 3 * D_HEADS + h * HEAD_DIM
        for j in range(n_blk):
            own = kv[r0:r0 + HEAD_DIM, j * width:(j + 1) * width].astype(BF)
            fxv_ref[0, h, j] = jnp.concatenate([own, other] if h % 2 == 0 else [other, own], axis=0)
    lf = _log_sigmoid(_dot_nt(wf_ref[...], n)[0:N_HEADS, :] + bf_ref[...])
    lf_ref[0] = lf

    @pl.when(i % tiles_per_batch == 0)
    def _():
        carry_ref[...] = jnp.zeros_like(carry_ref)

    tri = _tri(LANES, lambda r, c: r <= c)
    carry = carry_ref[...]
    for c in range(tm // LANES):
        hi, mid, lo = _split3(lf[:, c * LANES:(c + 1) * LANES])
        cs = _dot(hi, tri) + _dot(mid, tri) + _dot(lo, tri) + carry
        cum_ref[0, :, c * LANES:(c + 1) * LANES] = cs
        carry = jnp.broadcast_to(cs[:, LANES - 1:LANES], carry.shape)
    carry_ref[...] = carry


def _proj_prompt(h, mod, gain, wq, wkvt, wft, bf, batch, seq):
    m, d = h.shape
    tm = ROW_TILE
    tpb = seq // tm
    n_pair = D_HEADS // LANES
    t_spec = lambda rows: pl.BlockSpec((1, rows, tm), lambda i: (i // tpb, 0, i % tpb))
    t_shape = lambda rows: jax.ShapeDtypeStruct((batch, rows, seq), F32)
    b_spec = lambda n, w: pl.BlockSpec((1, n, tm // w, LANES, w), lambda i: (i // tpb, 0, i % tpb, 0, 0))
    b_shape = lambda n, w: jax.ShapeDtypeStruct((batch, n, seq // w, LANES, w), BF)
    tk = FOX_KEY_TILE
    return pl.pallas_call(
        functools.partial(_proj_prompt_kernel, tiles_per_batch=tpb),
        grid=(m // tm,),
        in_specs=[pl.BlockSpec((tm, d), lambda i: (i, 0)), _const_spec((1, d)),
                  _mod_spec(mod, tm, tpb, 3), _mod_spec(mod, tm, tpb, 4),
                  _const_spec(wq.shape), _const_spec(wkvt.shape), _const_spec(wft.shape),
                  _const_spec(bf.shape)],
        out_specs=[pl.BlockSpec((1, tm, wq.shape[1]), lambda i: (i // tpb, i % tpb, 0)),
                   t_spec(D_HEADS), t_spec(D_HEADS), t_spec(D_HEADS), t_spec(D_HEADS),
                   b_spec(2 * n_pair, ATT_TILE), b_spec(n_pair, tk), b_spec(N_HEADS, tk),
                   t_spec(N_HEADS), t_spec(N_HEADS)],
        out_shape=[jax.ShapeDtypeStruct((batch, seq, wq.shape[1]), BF),
                   t_shape(D_HEADS), t_shape(D_HEADS), t_shape(D_HEADS), t_shape(D_HEADS),
                   b_shape(2 * n_pair, ATT_TILE), b_shape(n_pair, tk), b_shape(N_HEADS, tk),
                   t_shape(N_HEADS), t_shape(N_HEADS)],
        scratch_shapes=[pltpu.VMEM((N_HEADS, LANES), F32)],
        compiler_params=_params(("arbitrary",)),
        name="proj_prompt",
    )(h, gain, mod, mod, wq, wkvt, wft, bf)


def _head_masks():
    lane = lax.broadcasted_iota(jnp.int32, (ATT_TILE, LANES), 1)
    return lane < HEAD_DIM


def _head_queries(q_ref):
    first = _head_masks()
    qs = []
    for pair in range(q_ref.shape[2] // LANES):
        q = q_ref[0, :, pair * LANES:(pair + 1) * LANES]
        zero = jnp.zeros_like(q)
        qs += [jnp.where(first, q, zero), jnp.where(first, zero, q)]
    return first, qs


def _skewed(n, stages):
    depth = len(stages)
    for step in range(n + depth - 1):
        for k in reversed(range(depth)):
            if 0 <= step - k < n:
                stages[k](step - k)


def _store_heads(o_ref, first, outs):
    for pair in range(len(outs) // 2):
        o_ref[0, :, pair * LANES:(pair + 1) * LANES] = jnp.where(first, outs[2 * pair], outs[2 * pair + 1]).astype(BF)


def _sb_prompt_kernel(q_ref, k_ref, v_ref, o_ref):
    qi = pl.program_id(2)
    t = ATT_TILE
    first, qs = _head_queries(q_ref)
    n_h = len(qs)
    tri = _tri(t, lambda r, c: r > c)
    tri2 = jnp.concatenate([tri, tri], axis=0)
    row = lax.broadcasted_iota(jnp.int32, (t, t), 0)
    col = lax.broadcasted_iota(jnp.int32, (t, t), 1)
    causal = col < row

    heads = range(n_h)

    def blocks(kbs, carries, accs, mask):
        carries, accs = list(carries), list(accs)
        items = [(kb, h) for kb in kbs for h in heads]
        zs = [_dot(qs[h], k_ref[0, h // 2, kb]) for kb, h in items]
        sps = [jnp.maximum(z, 0.0) + jnp.log2(1.0 + jnp.exp2(-jnp.abs(z))) for z in zs]
        logsig = [z - sp for z, sp in zip(zs, sps)]
        if mask is not None:
            sps = [jnp.where(mask, sp, 0.0) for sp in sps]
        rests = [_dot(jnp.concatenate(_split2(sp), axis=1), tri2) for sp in sps]
        totals = [jnp.sum(sp, axis=-1, keepdims=True) for sp in sps]
        pvs = []
        for i, (kb, h) in enumerate(items):
            a = jnp.exp2(logsig[i] - rests[i] - carries[h])
            if mask is not None:
                a = jnp.where(mask, a, 0.0)
            pvs.append(_dot_nt(a.astype(BF), v_ref[0, h // 2, kb]))
            carries[h] = carries[h] + totals[i]
        for i, (kb, h) in enumerate(items):
            accs[h] = accs[h] + pvs[i]
        return carries, accs

    carries, accs = blocks([qi], [jnp.zeros((t, 1), F32)] * n_h, [jnp.zeros((t, LANES), F32)] * n_h, causal)
    state = tuple(carries) + tuple(accs)

    def step(kbs, st):
        carries, accs = blocks(kbs, st[:n_h], st[n_h:], None)
        return tuple(carries) + tuple(accs)

    state = lax.cond(qi % 2 == 1, lambda st: step([qi - 1], st), lambda st: st, state)
    first_pair = qi - 1 - qi % 2
    state = lax.fori_loop(0, qi // 2, lambda j, st: step([first_pair - 2 * j, first_pair - 2 * j - 1], st), state)
    _store_heads(o_ref, first, list(state[n_h:]))


def _fox_prompt_kernel(q_ref, k_ref, v_ref, c_ref, o_ref):
    qi = pl.program_id(2)
    t = ATT_TILE
    tk = k_ref.shape[-1]
    per = tk // t
    kd = qi // per
    first, qs = _head_queries(q_ref)
    n_h = len(qs)
    row = lax.broadcasted_iota(jnp.int32, (t, tk), 0)
    col = lax.broadcasted_iota(jnp.int32, (t, tk), 1)
    causal = col <= row + (qi % per) * t
    bases = [c_ref[0, h, pl.ds(kd, 1), :][:, 0:1] for h in range(n_h)]

    def blocks(kb, ms, accs, mask):
        ms, accs = list(ms), list(accs)
        s, alpha, pv = {}, {}, {}

        def logits(h):
            s[h] = _dot(qs[h], k_ref[0, h // 2, kb]) - (c_ref[0, h, pl.ds(kb, 1), :] - bases[h]) * LOG2E
            if mask is not None:
                s[h] = jnp.where(mask, s[h], NEG_BIG)

        def row_max(h):
            m_new = jnp.maximum(ms[h], jnp.max(s[h], axis=-1, keepdims=True))
            alpha[h] = jnp.exp2(ms[h] - m_new)
            ms[h] = m_new

        def weights(h):
            pv[h] = _dot_nt(jnp.exp2(s[h] - ms[h]).astype(BF), v_ref[0, h, kb])

        def accumulate(h):
            accs[h] = alpha[h] * accs[h] + pv[h]

        _skewed(n_h, (logits, row_max, weights, accumulate))
        return ms, accs

    ms, accs = blocks(kd, [jnp.full((t, 1), NEG_BIG, F32)] * n_h, [jnp.zeros((t, LANES), F32)] * n_h, causal)

    def body(j, st):
        ms, accs = blocks(kd - 1 - j, st[:n_h], st[n_h:], None)
        return tuple(ms) + tuple(accs)

    state = lax.fori_loop(0, kd, body, tuple(ms) + tuple(accs))
    outs = []
    for h in range(n_h):
        acc = state[n_h + h]
        denom_lane = HEAD_DIM if h % 2 == 0 else 0
        outs.append(acc / acc[:, denom_lane:denom_lane + 1])
    _store_heads(o_ref, first, outs)


def _prompt_attention(q, sbb, fxk, fxv, cum, batch, seq):
    t = ATT_TILE
    pps = PAIRS_PER_STEP
    n_pair = D_HEADS // LANES
    n_grp = n_pair // pps
    grid = (batch, n_grp, seq // t)
    sem = ("arbitrary", "arbitrary", "arbitrary")
    width = pps * LANES

    def kv_spec(blocked, first_block, per_step=pps):
        return pl.BlockSpec((1, per_step) + blocked.shape[2:], lambda b, g, qi: (b, first_block + g, 0, 0, 0),
                            pipeline_mode=pl.Buffered(1))

    def q_spec(first_block):
        return pl.BlockSpec((1, t, width), lambda b, g, qi: (b, qi, first_block + g))

    out_spec = pl.BlockSpec((1, t, width), lambda b, g, qi: (b, qi, g))
    out_shape = jax.ShapeDtypeStruct((batch, seq, D_HEADS), BF)
    o_sb = pl.pallas_call(
        _sb_prompt_kernel,
        grid=grid,
        in_specs=[q_spec(0), kv_spec(sbb, 0), kv_spec(sbb, n_grp)],
        out_specs=out_spec,
        out_shape=out_shape,
        compiler_params=_params(sem),
        name="sb_prompt",
    )(q, sbb, sbb)
    o_fx = pl.pallas_call(
        _fox_prompt_kernel,
        grid=grid,
        in_specs=[q_spec(n_grp), kv_spec(fxk, 0), kv_spec(fxv, 0, 2 * pps),
                  pl.BlockSpec((1, 2 * pps) + cum.shape[2:], lambda b, g, qi: (b, g, 0, 0))],
        out_specs=out_spec,
        out_shape=out_shape,
        compiler_params=_params(sem),
        name="fox_prompt",
    )(q, fxk, fxv, cum)
    return o_sb, o_fx


def _post_kernel(h_ref, osb_ref, ofx_ref, gmix_ref, sh2_ref, sc2_ref, gt2_ref, gffn_ref, sh3_ref, sc3_ref,
                 gt3_ref, gfin_ref, wgs_ref, wgf_ref, wbs_ref, wbf_ref, wo_ref, wup_ref, wd_ref,
                 y_ref, acc_ref):
    h = h_ref[...]
    n = _norm_mod(h, gmix_ref[...], sh2_ref[0], sc2_ref[0]).astype(BF)
    merged = (_sigmoid(_dot(n, wgs_ref[...])) * _dot(osb_ref[...], wbs_ref[...])
              + _sigmoid(_dot(n, wgf_ref[...])) * _dot(ofx_ref[...], wbf_ref[...]))
    h2 = h + gt2_ref[0] * _dot(merged.astype(BF), wo_ref[...])
    xn = _norm_mod(h2, gffn_ref[...], sh3_ref[0], sc3_ref[0]).astype(BF)
    _swiglu_acc(xn, wup_ref, wd_ref, acc_ref)
    h3 = h2 + 0.5 * gt3_ref[0] * acc_ref[...]
    y_ref[...] = _rms(h3, gfin_ref[...])


def _post(h, o_sb, o_fx, mod, gmix, gffn, gfin, wgs, wgf, wbs, wbf, wo, wup, wd, tm, tiles_per_group):
    m, d = h.shape
    row = pl.BlockSpec((tm, d), lambda i: (i, 0))
    orow = pl.BlockSpec((tm, o_sb.shape[1]), lambda i: (i, 0))
    ms = lambda j: _mod_spec(mod, tm, tiles_per_group, j)
    return pl.pallas_call(
        _post_kernel,
        grid=(m // tm,),
        in_specs=[row, orow, orow, _const_spec((1, d)), ms(3), ms(4), ms(5), _const_spec((1, d)), ms(6), ms(7),
                  ms(8), _const_spec((1, d)),
                  _const_spec(wgs.shape), _const_spec(wgf.shape), _const_spec(wbs.shape), _const_spec(wbf.shape),
                  _const_spec(wo.shape), _const_spec(wup.shape), _const_spec(wd.shape)],
        out_specs=row,
        out_shape=jax.ShapeDtypeStruct((m, d), F32),
        scratch_shapes=[pltpu.VMEM((tm, d), F32)],
        compiler_params=_params(("arbitrary",)),
        name="post",
    )(h, o_sb, o_fx, gmix, mod, mod, mod, gffn, mod, mod, mod, gfin, wgs, wgf, wbs, wbf, wo, wup, wd)


def _proj_sample_kernel(h_ref, g_ref, sh_ref, sc_ref, w_ref, wf_ref, bf_ref, p_ref, lf_ref):
    n = _norm_mod(h_ref[...], g_ref[...], sh_ref[0], sc_ref[0]).astype(BF)
    p_ref[...] = _dot(n, w_ref[...])
    lf_ref[...] = _log_sigmoid(_dot_nt(n, wf_ref[...])[:, 0:N_HEADS] + bf_ref[...])


def _proj_sample(h, mod, gain, w, wft, bf_row):
    m, d = h.shape
    return pl.pallas_call(
        _proj_sample_kernel,
        grid=(1,),
        in_specs=[_const_spec((m, d)), _const_spec((1, d)), _mod_spec(mod, m, 1, 3), _mod_spec(mod, m, 1, 4),
                  _const_spec(w.shape), _const_spec(wft.shape), _const_spec(bf_row.shape)],
        out_specs=[_const_spec((m, w.shape[1])), _const_spec((m, N_HEADS))],
        out_shape=[jax.ShapeDtypeStruct((m, w.shape[1]), F32), jax.ShapeDtypeStruct((m, N_HEADS), F32)],
        compiler_params=_params(("arbitrary",)),
        name="proj_sample",
    )(h, gain, mod, mod, w, wft, bf_row)


def _decode_kernel(pt_ref, qb_ref, *rest, fox, n_steps):
    if fox:
        (knew_ref, vnew_ref, lfnew_ref, k_hbm, v_hbm, lf_hbm, o_ref,
         kbuf, vbuf, lfbuf, zbuf, abuf, acc_ref, carry_ref, m_ref, l_ref, sem) = rest
    else:
        k_hbm, v_hbm, o_ref, kbuf, vbuf, zbuf, abuf, acc_ref, carry_ref, sem = rest
    b = pl.program_id(0)
    c = pl.program_id(1)
    step = b * n_steps + c
    total = pl.num_programs(0) * n_steps
    ch = PAGES_PER_STEP

    def copies(s, slot):
        sb = s // n_steps
        first_page = (n_steps - 1 - s % n_steps) * ch
        out = []
        for p in range(ch):
            page = pt_ref[sb, first_page + p]
            out.append(pltpu.make_async_copy(k_hbm.at[page], kbuf.at[slot, p], sem.at[0, slot]))
            out.append(pltpu.make_async_copy(v_hbm.at[page], vbuf.at[slot, p], sem.at[1, slot]))
            if fox:
                out.append(pltpu.make_async_copy(lf_hbm.at[page], lfbuf.at[slot, p], sem.at[2, slot]))
        return out

    slot = step % 2

    @pl.when(step == 0)
    def _():
        for cp in copies(step, slot):
            cp.start()

    @pl.when(step + 1 < total)
    def _():
        for cp in copies(step + 1, 1 - slot):
            cp.start()

    sel_r =lax.broadcasted_iota(jnp.int32, (N_HEADS, 2 * N_HEADS * 8), 0)
    sel_c = lax.broadcasted_iota(jnp.int32, (N_HEADS, 2 * N_HEADS * 8), 1)
    sel = jnp.where((sel_c % (N_HEADS * 8)) // 8 == sel_r, 1.0, 0.0).astype(BF)

    def scores(page_ref):
        parts = []
        for h in range(N_HEADS):
            part = page_ref[h, 0:8, :] * qb_ref[0, h, 0:8, :]
            for g in range(1, HEAD_DIM // 8):
                part = part + page_ref[h, g * 8:(g + 1) * 8, :] * qb_ref[0, h, g * 8:(g + 1) * 8, :]
            parts.append(part)
        stack = jnp.concatenate(parts, axis=0)
        hi, lo = _split2(stack)
        return _dot(sel, jnp.concatenate([hi, lo], axis=0))

    @pl.when(c == 0)
    def _():
        if fox:
            m_ref[...] = scores(knew_ref.at[0]) * (QK_SCALE * LOG2E)
            l_ref[...] = jnp.ones_like(l_ref)
            carry_ref[...] = jnp.broadcast_to(lfnew_ref[0], carry_ref.shape)
            for h in range(N_HEADS):
                acc_ref[h] = jnp.where(lax.broadcasted_iota(jnp.int32, (HEAD_DIM, LANES), 1) == 0,
                                       vnew_ref[0, h], 0.0)
        else:
            carry_ref[...] = jnp.zeros_like(carry_ref)
            acc_ref[...] = jnp.zeros_like(acc_ref)

    for cp in copies(step, slot):
        cp.wait()

    for p in range(ch):
        zbuf[p] = scores(kbuf.at[slot, p])

    carry = carry_ref[...]
    tri = _tri(LANES, lambda r, cc: r > cc)
    if fox:
        tri3 = jnp.concatenate([tri, tri, tri], axis=0)
        logits = [None] * ch
        for p in reversed(range(ch)):
            lf = lfbuf[slot, p]
            hi, mid, lo = _split3(lf)
            suffix = _dot(jnp.concatenate([hi, mid, lo], axis=1), tri3)
            logits[p] = (zbuf[p] * QK_SCALE + suffix + carry) * LOG2E
            carry = carry + jnp.sum(lf, axis=-1, keepdims=True)
        mx = logits[0]
        for p in range(1, ch):
            mx = jnp.maximum(mx, logits[p])
        m_old = m_ref[...]
        m_new = jnp.maximum(m_old, jnp.max(mx, axis=-1, keepdims=True))
        alpha = jnp.exp2(m_old - m_new)
        lsum = jnp.zeros((N_HEADS, LANES), F32)
        for p in range(ch):
            a = jnp.exp2(logits[p] - m_new)
            abuf[p] = a
            lsum = lsum + a
        l_ref[...] = alpha * l_ref[...] + jnp.sum(lsum, axis=-1, keepdims=True)
        m_ref[...] = m_new
    else:
        tri2 = jnp.concatenate([tri, tri], axis=0)
        for p in reversed(range(ch)):
            z = zbuf[p] * (QK_SCALE * LOG2E)
            sp = jnp.maximum(z, 0.0) + jnp.log2(1.0 + jnp.exp2(-jnp.abs(z)))
            hi, lo = _split2(sp)
            rest_ = _dot(jnp.concatenate([hi, lo], axis=1), tri2)
            abuf[p] = jnp.exp2(z - sp - rest_ - carry)
            carry = carry + jnp.sum(sp, axis=-1, keepdims=True)
    carry_ref[...] = carry

    for h in range(N_HEADS):
        acc = acc_ref[h]
        if fox:
            acc = acc * alpha[h:h + 1, :]
        for p in range(ch):
            acc = acc + abuf[p, h:h + 1, :] * vbuf[slot, p, h]
        acc_ref[h] = acc

    @pl.when(c == n_steps - 1)
    def _():
        for h in range(N_HEADS):
            acc = acc_ref[h]
            if fox:
                acc = acc / l_ref[h:h + 1, :]
            o_ref[0, h] = jnp.sum(acc, axis=-1, keepdims=True)


def _decode(page_table, qb, k_cache, v_cache, fox_args=None):
    n, n_pages = page_table.shape
    fox = fox_args is not None
    ch = PAGES_PER_STEP
    n_steps = n_pages // ch
    page = k_cache.shape[1:]
    seq_spec = pl.BlockSpec((1,) + qb.shape[1:], lambda b, c, pt: (b, 0, 0, 0))
    any_spec = pl.BlockSpec(memory_space=pl.ANY)
    in_specs = [seq_spec]
    args = [qb]
    scratch = [pltpu.VMEM((2, ch) + page, F32), pltpu.VMEM((2, ch) + page, F32)]
    if fox:
        knew, vnew, lfnew, lf_cache = fox_args
        in_specs += [seq_spec, seq_spec, pl.BlockSpec((1, N_HEADS, 1), lambda b, c, pt: (b, 0, 0))]
        args += [knew, vnew, lfnew]
        scratch.append(pltpu.VMEM((2, ch, N_HEADS, LANES), F32))
    in_specs += [any_spec, any_spec]
    args += [k_cache, v_cache]
    if fox:
        in_specs.append(any_spec)
        args.append(lf_cache)
    scratch += [pltpu.VMEM((ch, N_HEADS, LANES), F32), pltpu.VMEM((ch, N_HEADS, LANES), F32),
                pltpu.VMEM((N_HEADS, HEAD_DIM, LANES), F32), pltpu.VMEM((N_HEADS, LANES), F32)]
    if fox:
        scratch += [pltpu.VMEM((N_HEADS, LANES), F32), pltpu.VMEM((N_HEADS, LANES), F32)]
    scratch.append(pltpu.SemaphoreType.DMA((3, 2)))
    return pl.pallas_call(
        functools.partial(_decode_kernel, fox=fox, n_steps=n_steps),
        grid_spec=pltpu.PrefetchScalarGridSpec(
            num_scalar_prefetch=1,
            grid=(n, n_steps),
            in_specs=in_specs,
            out_specs=pl.BlockSpec((1, N_HEADS, HEAD_DIM, 1), lambda b, c, pt: (b, 0, 0, 0)),
            scratch_shapes=scratch),
        out_shape=jax.ShapeDtypeStruct((n, N_HEADS, HEAD_DIM, 1), F32),
        compiler_params=_params(("arbitrary", "arbitrary")),
        name="fox_decode" if fox else "sb_decode",
    )(page_table, *args)


def _ffn_weights(w_up, w_down):
    assert w_down.shape[0] % FF_CHUNK == 0
    return w_up.astype(BF), w_down.astype(BF)


def _lane_rep(x):
    n = x.shape[0]
    return jnp.broadcast_to(x.reshape(n, N_HEADS, HEAD_DIM, 1), (n, N_HEADS, HEAD_DIM, LANES))


def kernel(x_prompt, x_sample, c_prompt, c_sample, cache_sb_k, cache_sb_v, cache_fox_k, cache_fox_v, cache_fox_logf, page_table, norm_ffn1, norm_mix, norm_ffn2, w_ada, b_ada, w_up1, w_down1, w_up2, w_down2, w_in, b_forget, w_br_sb, w_br_fox, w_out, norm_final):
    batch, seq, d = x_prompt.shape
    n_dec = x_sample.shape[0]
    depth = w_ada.shape[0]
    assert depth == 1 and x_sample.shape[1] == 1
    assert seq % ROW_TILE == 0 and ROW_TILE % ATT_TILE == 0
    assert page_table.shape[1] % PAGES_PER_STEP == 0 and cache_sb_k.shape[2] == LANES

    wt = jnp.transpose(w_in[0])
    dh = D_HEADS
    wq = jnp.concatenate([wt[0:dh], wt[3 * dh:4 * dh]], axis=0).T.astype(BF)
    wkvt = jnp.concatenate([wt[dh:3 * dh], wt[4 * dh:6 * dh]], axis=0).astype(BF)
    wqkv_s = wt[0:6 * dh].T.astype(BF)
    wft = jnp.concatenate([wt[6 * dh:6 * dh + N_HEADS], jnp.zeros((N_HEADS, d), F32)], axis=0).astype(BF)
    wgs = wt[6 * dh + N_HEADS:6 * dh + N_HEADS + d].T.astype(BF)
    wgf = wt[6 * dh + N_HEADS + d:].T.astype(BF)
    wbs, wbf, wo = w_br_sb[0].astype(BF), w_br_fox[0].astype(BF), w_out[0].astype(BF)
    ffn1_w = _ffn_weights(w_up1[0], w_down1[0])
    ffn2_w = _ffn_weights(w_up2[0], w_down2[0])
    g1, gm, g2, gf = norm_ffn1, norm_mix, norm_ffn2, norm_final.reshape(1, d)
    bf_col = b_forget.reshape(N_HEADS, 1)
    bf_row = b_forget.reshape(1, N_HEADS)

    n_c = batch + n_dec
    n_pad = -n_c % 8
    c_all = jnp.concatenate([c_prompt, c_sample, jnp.zeros((n_pad, d), F32)], axis=0)
    mod = _adaln(c_all, w_ada[0], b_ada)
    mod_p = mod[:batch].reshape(batch, 1, N_MOD * d)
    mod_s = mod[batch:n_c].reshape(1, n_dec, N_MOD * d)

    tpb = seq // ROW_TILE
    xp = x_prompt.reshape(batch * seq, d)
    h = _ffn(xp, mod_p, g1, *ffn1_w, ROW_TILE, tpb, 0)
    q, ksb_t, vsb_t, kfx_t, vfx_t, sbb, fxk, fxv, lft, cum = _proj_prompt(h, mod_p, gm, wq, wkvt, wft, bf_col,
                                                                          batch, seq)
    cum = cum.reshape(batch, N_HEADS, seq // FOX_KEY_TILE, FOX_KEY_TILE)
    o_sb, o_fx = _prompt_attention(q, sbb, fxk, fxv, cum, batch, seq)
    y_prompt = _post(h, o_sb.reshape(batch * seq, dh), o_fx.reshape(batch * seq, dh), mod_p, gm, g2, gf,
                     wgs, wgf, wbs, wbf, wo, *ffn2_w, ROW_TILE, tpb).reshape(batch, seq, d)

    def state(t):
        return jnp.transpose(t.reshape(batch, N_HEADS, HEAD_DIM, seq), (0, 3, 1, 2))[None]

    lf_prompt = jnp.transpose(lft, (0, 2, 1))[None]

    xs = x_sample.reshape(n_dec, d)
    hs = _ffn(xs, mod_s, g1, *ffn1_w, n_dec, 1, 0)
    ps, lfs = _proj_sample(hs, mod_s, gm, wqkv_s, wft, bf_row)
    q_sb, k_sb, v_sb, q_fx, k_fx, v_fx = [ps[:, i * dh:(i + 1) * dh] for i in range(6)]
    view = lambda cache: jnp.transpose(cache[0], (0, 2, 3, 1))
    o_sb_s = _decode(page_table, _lane_rep(q_sb), view(cache_sb_k), view(cache_sb_v))
    lf_view = jnp.transpose(cache_fox_logf[0], (0, 2, 1))
    o_fx_s = _decode(page_table, _lane_rep(q_fx), view(cache_fox_k), view(cache_fox_v),
                     (_lane_rep(k_fx), _lane_rep(v_fx), lfs.reshape(n_dec, N_HEADS, 1), lf_view))
    y_sample = _post(hs, o_sb_s.reshape(n_dec, dh).astype(BF), o_fx_s.reshape(n_dec, dh).astype(BF), mod_s,
                     gm, g2, gf, wgs, wgf, wbs, wbf, wo, *ffn2_w, n_dec, 1).reshape(n_dec, 1, d)

    st = lambda x: x.reshape(1, n_dec, 1, N_HEADS, HEAD_DIM)
    return (y_prompt, y_sample, state(ksb_t), state(vsb_t), state(kfx_t), state(vfx_t), lf_prompt,
            st(k_sb), st(v_sb), st(k_fx), st(v_fx), lfs.reshape(1, n_dec, 1, N_HEADS))
```

```python
import functools

import jax
import jax.numpy as jnp
from jax import lax
from jax.experimental import pallas as pl
from jax.experimental.pallas import tpu as pltpu

HEAD_DIM = 64
N_HEADS = 8
D_HEADS = N_HEADS * HEAD_DIM
N_MOD = 9
EPS = 1e-6
LOG2E = 1.4426950408889634
QK_SCALE = HEAD_DIM ** -0.5
NEG_BIG = -1e30

LANES = 128
FF_CHUNK = 256
ROW_TILE = 512
ATT_TILE = 256
FOX_KEY_TILE = 512
PAIRS_PER_STEP = 4
PAGES_PER_STEP = 16
VMEM_LIMIT = 56 * 1024 * 1024

BF = jnp.bfloat16
F32 = jnp.float32


def _dot(a, b):
    return jnp.dot(a, b, preferred_element_type=F32)


def _dot_nt(a, b):
    return lax.dot_general(a, b, (((1,), (1,)), ((), ())), preferred_element_type=F32)


def _split2(x):
    hi = x.astype(BF)
    lo = (x - hi.astype(F32)).astype(BF)
    return hi, lo


def _split3(x):
    hi = x.astype(BF)
    r = x - hi.astype(F32)
    mid = r.astype(BF)
    lo = (r - mid.astype(F32)).astype(BF)
    return hi, mid, lo


def _sigmoid(x):
    return 1.0 / (1.0 + jnp.exp(-x))


def _rms(x, g):
    ms = jnp.mean(x * x, axis=-1, keepdims=True)
    return x * lax.rsqrt(ms + EPS) * g


def _norm_mod(x, g, shift, scale):
    return _rms(x, g) * (1.0 + scale) + shift


def _tri(n, fn):
    r = lax.broadcasted_iota(jnp.int32, (n, n), 0)
    c = lax.broadcasted_iota(jnp.int32, (n, n), 1)
    return jnp.where(fn(r, c), 1.0, 0.0).astype(BF)


def _const_spec(shape):
    nd = len(shape)
    return pl.BlockSpec(shape, lambda *_: (0,) * nd, pipeline_mode=pl.Buffered(1))


def _params(sem):
    return pltpu.CompilerParams(dimension_semantics=sem, vmem_limit_bytes=VMEM_LIMIT)


def _adaln_kernel(c_ref, w_ref, b_ref, o_ref):
    c = c_ref[...]
    s = (c * _sigmoid(c)).astype(BF)
    o_ref[...] = _dot(s, w_ref[...].astype(BF)) + b_ref[...]


def _adaln(c, w, b):
    n, d = c.shape
    cols = w.shape[1]
    blk = d
    return pl.pallas_call(
        _adaln_kernel,
        grid=(cols // blk,),
        in_specs=[pl.BlockSpec((n, d), lambda j: (0, 0)),
                  pl.BlockSpec((d, blk), lambda j: (0, j)),
                  pl.BlockSpec((1, blk), lambda j: (0, j))],
        out_specs=pl.BlockSpec((n, blk), lambda j: (0, j)),
        out_shape=jax.ShapeDtypeStruct((n, cols), F32),
        compiler_params=_params(("arbitrary",)),
        name="adaln",
    )(c, w, b)


def _swiglu_acc(xn, wup_ref, wd_ref, acc_ref):
    ff = wd_ref.shape[0]
    for c in range(ff // FF_CHUNK):
        lo = c * FF_CHUNK
        g = _dot(xn, wup_ref[:, lo:lo + FF_CHUNK])
        u = _dot(xn, wup_ref[:, ff + lo:ff + lo + FF_CHUNK])
        a = (g * _sigmoid(g) * u).astype(BF)
        part = _dot(a, wd_ref[lo:lo + FF_CHUNK, :])
        if c == 0:
            acc_ref[...] = part
        else:
            acc_ref[...] += part


def _ffn_kernel(x_ref, g_ref, sh_ref, sc_ref, gt_ref, wup_ref, wd_ref, o_ref, acc_ref):
    x = x_ref[...]
    xn = _norm_mod(x, g_ref[...], sh_ref[0], sc_ref[0]).astype(BF)
    _swiglu_acc(xn, wup_ref, wd_ref, acc_ref)
    o_ref[...] = x + 0.5 * gt_ref[0] * acc_ref[...]


def _mod_spec(mod, tm, tiles_per_group, j):
    lm = mod.shape[1]
    d = mod.shape[2] // N_MOD
    if lm == 1:
        return pl.BlockSpec((1, 1, d), lambda i: (i // tiles_per_group, 0, j))
    return pl.BlockSpec((1, tm, d), lambda i: (0, i, j))


def _ffn(x, mod, gain, wup, wd, tm, tiles_per_group, first_mod):
    m, d = x.shape
    row = pl.BlockSpec((tm, d), lambda i: (i, 0))
    return pl.pallas_call(
        _ffn_kernel,
        grid=(m // tm,),
        in_specs=[row, _const_spec((1, d)),
                  _mod_spec(mod, tm, tiles_per_group, first_mod),
                  _mod_spec(mod, tm, tiles_per_group, first_mod + 1),
                  _mod_spec(mod, tm, tiles_per_group, first_mod + 2),
                  _const_spec(wup.shape), _const_spec(wd.shape)],
        out_specs=row,
        out_shape=jax.ShapeDtypeStruct((m, d), F32),
        scratch_shapes=[pltpu.VMEM((tm, d), F32)],
        compiler_params=_params(("arbitrary",)),
        name="ffn1",
    )(x, gain, mod, mod, mod, wup, wd)


def _log_sigmoid(x):
    return jnp.minimum(x, 0.0) - jnp.log1p(jnp.exp(-jnp.abs(x)))


def _proj_prompt_kernel(h_ref, g_ref, sh_ref, sc_ref, wq_ref, wkv_ref, wf_ref, bf_ref,
                        q_ref, ksb_ref, vsb_ref, kfx_ref, vfx_ref, sbb_ref, fxk_ref, fxv_ref, lf_ref, cum_ref,
                        carry_ref, *, tiles_per_batch):
    i = pl.program_id(0)
    tm = h_ref.shape[0]
    n = _norm_mod(h_ref[...], g_ref[...], sh_ref[0], sc_ref[0]).astype(BF)
    q_ref[0] = (_dot(n, wq_ref[...]) * (QK_SCALE * LOG2E)).astype(BF)
    kv = _dot_nt(wkv_ref[...], n)
    for s, ref in enumerate((ksb_ref, vsb_ref, kfx_ref, vfx_ref)):
        ref[0] = kv[s * D_HEADS:(s + 1) * D_HEADS, :]

    def pair_tiles(ref, first_row):
        n_grp, n_blk, _, width = ref.shape[1:]
        for g in range(n_grp):
            r0 = first_row + g * LANES
            for j in range(n_blk):
                ref[0, g, j] = kv[r0:r0 + LANES, j * width:(j + 1) * width].astype(BF)

    pair_tiles(sbb_ref, 0)
    pair_tiles(fxk_ref, 2 * D_HEADS)
    n_blk, _, width = fxv_ref.shape[2:]
    one_row = jnp.where(lax.broadcasted_iota(jnp.int32, (8, width), 0) == 0, 1.0, 0.0).astype(BF)
    other = jnp.concatenate([one_row, jnp.zeros((HEAD_DIM - 8, width), BF)], axis=0)
    for h in range(N_HEADS):
        r0 = 3 * D_HEADS + h * HEAD_DIM
        for j in range(n_blk):
            own = kv[r0:r0 + HEAD_DIM, j * width:(j + 1) * width].astype(BF)
            fxv_ref[0, h, j] = jnp.concatenate([own, other] if h % 2 == 0 else [other, own], axis=0)
    lf = _log_sigmoid(_dot_nt(wf_ref[...], n)[0:N_HEADS, :] + bf_ref[...])
    lf_ref[0] = lf

    @pl.when(i % tiles_per_batch == 0)
    def _():
        carry_ref[...] = jnp.zeros_like(carry_ref)

    tri = _tri(LANES, lambda r, c: r <= c)
    carry = carry_ref[...]
    for c in range(tm // LANES):
        hi, mid, lo = _split3(lf[:, c * LANES:(c + 1) * LANES])
        cs = _dot(hi, tri) + _dot(mid, tri) + _dot(lo, tri) + carry
        cum_ref[0, :, c * LANES:(c + 1) * LANES] = cs
        carry = jnp.broadcast_to(cs[:, LANES - 1:LANES], carry.shape)
    carry_ref[...] = carry


def _proj_prompt(h, mod, gain, wq, wkvt, wft, bf, batch, seq):
    m, d = h.shape
    tm = ROW_TILE
    tpb = seq // tm
    n_pair = D_HEADS // LANES
    t_spec = lambda rows: pl.BlockSpec((1, rows, tm), lambda i: (i // tpb, 0, i % tpb))
    t_shape = lambda rows: jax.ShapeDtypeStruct((batch, rows, seq), F32)
    b_spec = lambda n, w: pl.BlockSpec((1, n, tm // w, LANES, w), lambda i: (i // tpb, 0, i % tpb, 0, 0))
    b_shape = lambda n, w: jax.ShapeDtypeStruct((batch, n, seq // w, LANES, w), BF)
    tk = FOX_KEY_TILE
    return pl.pallas_call(
        functools.partial(_proj_prompt_kernel, tiles_per_batch=tpb),
        grid=(m // tm,),
        in_specs=[pl.BlockSpec((tm, d), lambda i: (i, 0)), _const_spec((1, d)),
                  _mod_spec(mod, tm, tpb, 3), _mod_spec(mod, tm, tpb, 4),
                  _const_spec(wq.shape), _const_spec(wkvt.shape), _const_spec(wft.shape),
                  _const_spec(bf.shape)],
        out_specs=[pl.BlockSpec((1, tm, wq.shape[1]), lambda i: (i // tpb, i % tpb, 0)),
                   t_spec(D_HEADS), t_spec(D_HEADS), t_spec(D_HEADS), t_spec(D_HEADS),
                   b_spec(2 * n_pair, ATT_TILE), b_spec(n_pair, tk), b_spec(N_HEADS, tk),
                   t_spec(N_HEADS), t_spec(N_HEADS)],
        out_shape=[jax.ShapeDtypeStruct((batch, seq, wq.shape[1]), BF),
                   t_shape(D_HEADS), t_shape(D_HEADS), t_shape(D_HEADS), t_shape(D_HEADS),
                   b_shape(2 * n_pair, ATT_TILE), b_shape(n_pair, tk), b_shape(N_HEADS, tk),
                   t_shape(N_HEADS), t_shape(N_HEADS)],
        scratch_shapes=[pltpu.VMEM((N_HEADS, LANES), F32)],
        compiler_params=_params(("arbitrary",)),
        name="proj_prompt",
    )(h, gain, mod, mod, wq, wkvt, wft, bf)


def _head_masks():
    lane = lax.broadcasted_iota(jnp.int32, (ATT_TILE, LANES), 1)
    return lane < HEAD_DIM


def _head_queries(q_ref):
    first = _head_masks()
    qs = []
    for pair in range(q_ref.shape[2] // LANES):
        q = q_ref[0, :, pair * LANES:(pair + 1) * LANES]
        zero = jnp.zeros_like(q)
        qs += [jnp.where(first, q, zero), jnp.where(first, zero, q)]
    return first, qs


def _skewed(n, stages):
    depth = len(stages)
    for step in range(n + depth - 1):
        for k in reversed(range(depth)):
            if 0 <= step - k < n:
                stages[k](step - k)


def _store_heads(o_ref, first, outs):
    for pair in range(len(outs) // 2):
        o_ref[0, :, pair * LANES:(pair + 1) * LANES] = jnp.where(first, outs[2 * pair], outs[2 * pair + 1]).astype(BF)


def _sb_prompt_kernel(q_ref, k_ref, v_ref, o_ref):
    qi = pl.program_id(2)
    t = ATT_TILE
    first, qs = _head_queries(q_ref)
    n_h = len(qs)
    tri = _tri(t, lambda r, c: r > c)
    row = lax.broadcasted_iota(jnp.int32, (t, t), 0)
    col = lax.broadcasted_iota(jnp.int32, (t, t), 1)
    causal = col < row

    heads = range(n_h)

    def blocks(kbs, carries, accs, mask):
        carries, accs = list(carries), list(accs)
        items = [(kb, h) for kb in kbs for h in heads]
        zs = [_dot(qs[h], k_ref[0, h // 2, kb]) for kb, h in items]
        sps = [jnp.maximum(z, 0.0) + jnp.log2(1.0 + jnp.exp2(-jnp.abs(z))) for z in zs]
        logsig = [z - sp for z, sp in zip(zs, sps)]
        if mask is not None:
            sps = [jnp.where(mask, sp, 0.0) for sp in sps]
        rests = [_dot(sp.astype(BF), tri) for sp in sps]
        totals = [jnp.sum(sp, axis=-1, keepdims=True) for sp in sps]
        pvs = []
        for i, (kb, h) in enumerate(items):
            a = jnp.exp2(logsig[i] - rests[i] - carries[h])
            if mask is not None:
                a = jnp.where(mask, a, 0.0)
            pvs.append(_dot_nt(a.astype(BF), v_ref[0, h // 2, kb]))
            carries[h] = carries[h] + totals[i]
        for i, (kb, h) in enumerate(items):
            accs[h] = accs[h] + pvs[i]
        return carries, accs

    carries, accs = blocks([qi], [jnp.zeros((t, 1), F32)] * n_h, [jnp.zeros((t, LANES), F32)] * n_h, causal)
    state = tuple(carries) + tuple(accs)

    def step(kbs, st):
        carries, accs = blocks(kbs, st[:n_h], st[n_h:], None)
        return tuple(carries) + tuple(accs)

    state = lax.cond(qi % 2 == 1, lambda st: step([qi - 1], st), lambda st: st, state)
    first_pair = qi - 1 - qi % 2
    state = lax.fori_loop(0, qi // 2, lambda j, st: step([first_pair - 2 * j, first_pair - 2 * j - 1], st), state)
    _store_heads(o_ref, first, list(state[n_h:]))


def _fox_prompt_kernel(q_ref, k_ref, v_ref, c_ref, o_ref):
    qi = pl.program_id(2)
    t = ATT_TILE
    tk = k_ref.shape[-1]
    per = tk // t
    kd = qi // per
    first, qs = _head_queries(q_ref)
    n_h = len(qs)
    row = lax.broadcasted_iota(jnp.int32, (t, tk), 0)
    col = lax.broadcasted_iota(jnp.int32, (t, tk), 1)
    causal = col <= row + (qi % per) * t
    bases = [c_ref[0, h, pl.ds(kd, 1), :][:, 0:1] for h in range(n_h)]

    def blocks(kbs, ms, accs, mask):
        ms, accs = list(ms), list(accs)
        items = [(kb, h) for kb in kbs for h in range(n_h)]
        s, mrow, alpha, pv = {}, {}, {}, {}

        def logits(i):
            kb, h = items[i]
            s[i] = _dot(qs[h], k_ref[0, h // 2, kb]) - (c_ref[0, h, pl.ds(kb, 1), :] - bases[h]) * LOG2E
            if mask is not None:
                s[i] = jnp.where(mask, s[i], NEG_BIG)

        def row_max(i):
            h = items[i][1]
            mrow[i] = jnp.maximum(ms[h], jnp.max(s[i], axis=-1, keepdims=True))
            alpha[i] = jnp.exp2(ms[h] - mrow[i])
            ms[h] = mrow[i]

        def weights(i):
            kb, h = items[i]
            pv[i] = _dot_nt(jnp.exp2(s[i] - mrow[i]).astype(BF), v_ref[0, h, kb])

        def accumulate(i):
            h = items[i][1]
            accs[h] = alpha[i] * accs[h] + pv[i]

        _skewed(len(items), (logits, row_max, weights, accumulate))
        return tuple(ms) + tuple(accs)

    state = blocks([kd], [jnp.full((t, 1), NEG_BIG, F32)] * n_h, [jnp.zeros((t, LANES), F32)] * n_h, causal)
    state = lax.cond(kd % 2 == 1, lambda st: blocks([kd - 1], st[:n_h], st[n_h:], None), lambda st: st, state)
    first_pair = kd - 1 - kd % 2
    state = lax.fori_loop(
        0, kd // 2,
        lambda j, st: blocks([first_pair - 2 * j, first_pair - 2 * j - 1], st[:n_h], st[n_h:], None), state)
    outs = []
    for h in range(n_h):
        acc = state[n_h + h]
        denom_lane = HEAD_DIM if h % 2 == 0 else 0
        outs.append(acc / acc[:, denom_lane:denom_lane + 1])
    _store_heads(o_ref, first, outs)


def _prompt_attention(q, sbb, fxk, fxv, cum, batch, seq):
    t = ATT_TILE
    pps = PAIRS_PER_STEP
    n_pair = D_HEADS // LANES
    n_grp = n_pair // pps
    grid = (batch, n_grp, seq // t)
    sem = ("arbitrary", "arbitrary", "arbitrary")
    width = pps * LANES

    def kv_spec(blocked, first_block, per_step=pps):
        return pl.BlockSpec((1, per_step) + blocked.shape[2:], lambda b, g, qi: (b, first_block + g, 0, 0, 0),
                            pipeline_mode=pl.Buffered(1))

    def q_spec(first_block):
        return pl.BlockSpec((1, t, width), lambda b, g, qi: (b, qi, first_block + g))

    out_spec = pl.BlockSpec((1, t, width), lambda b, g, qi: (b, qi, g))
    out_shape = jax.ShapeDtypeStruct((batch, seq, D_HEADS), BF)
    o_sb = pl.pallas_call(
        _sb_prompt_kernel,
        grid=grid,
        in_specs=[q_spec(0), kv_spec(sbb, 0), kv_spec(sbb, n_grp)],
        out_specs=out_spec,
        out_shape=out_shape,
        compiler_params=_params(sem),
        name="sb_prompt",
    )(q, sbb, sbb)
    o_fx = pl.pallas_call(
        _fox_prompt_kernel,
        grid=grid,
        in_specs=[q_spec(n_grp), kv_spec(fxk, 0), kv_spec(fxv, 0, 2 * pps),
                  pl.BlockSpec((1, 2 * pps) + cum.shape[2:], lambda b, g, qi: (b, g, 0, 0))],
        out_specs=out_spec,
        out_shape=out_shape,
        compiler_params=_params(sem),
        name="fox_prompt",
    )(q, fxk, fxv, cum)
    return o_sb, o_fx


def _post_kernel(h_ref, osb_ref, ofx_ref, gmix_ref, sh2_ref, sc2_ref, gt2_ref, gffn_ref, sh3_ref, sc3_ref,
                 gt3_ref, gfin_ref, wgs_ref, wgf_ref, wbs_ref, wbf_ref, wo_ref, wup_ref, wd_ref,
                 y_ref, acc_ref):
    h = h_ref[...]
    n = _norm_mod(h, gmix_ref[...], sh2_ref[0], sc2_ref[0]).astype(BF)
    merged = (_sigmoid(_dot(n, wgs_ref[...])) * _dot(osb_ref[...], wbs_ref[...])
              + _sigmoid(_dot(n, wgf_ref[...])) * _dot(ofx_ref[...], wbf_ref[...]))
    h2 = h + gt2_ref[0] * _dot(merged.astype(BF), wo_ref[...])
    xn = _norm_mod(h2, gffn_ref[...], sh3_ref[0], sc3_ref[0]).astype(BF)
    _swiglu_acc(xn, wup_ref, wd_ref, acc_ref)
    h3 = h2 + 0.5 * gt3_ref[0] * acc_ref[...]
    y_ref[...] = _rms(h3, gfin_ref[...])


def _post(h, o_sb, o_fx, mod, gmix, gffn, gfin, wgs, wgf, wbs, wbf, wo, wup, wd, tm, tiles_per_group):
    m, d = h.shape
    row = pl.BlockSpec((tm, d), lambda i: (i, 0))
    orow = pl.BlockSpec((tm, o_sb.shape[1]), lambda i: (i, 0))
    ms = lambda j: _mod_spec(mod, tm, tiles_per_group, j)
    return pl.pallas_call(
        _post_kernel,
        grid=(m // tm,),
        in_specs=[row, orow, orow, _const_spec((1, d)), ms(3), ms(4), ms(5), _const_spec((1, d)), ms(6), ms(7),
                  ms(8), _const_spec((1, d)),
                  _const_spec(wgs.shape), _const_spec(wgf.shape), _const_spec(wbs.shape), _const_spec(wbf.shape),
                  _const_spec(wo.shape), _const_spec(wup.shape), _const_spec(wd.shape)],
        out_specs=row,
        out_shape=jax.ShapeDtypeStruct((m, d), F32),
        scratch_shapes=[pltpu.VMEM((tm, d), F32)],
        compiler_params=_params(("arbitrary",)),
        name="post",
    )(h, o_sb, o_fx, gmix, mod, mod, mod, gffn, mod, mod, mod, gfin, wgs, wgf, wbs, wbf, wo, wup, wd)


def _proj_sample_kernel(h_ref, g_ref, sh_ref, sc_ref, w_ref, wf_ref, bf_ref, p_ref, lf_ref):
    n = _norm_mod(h_ref[...], g_ref[...], sh_ref[0], sc_ref[0]).astype(BF)
    p_ref[...] = _dot(n, w_ref[...])
    lf_ref[...] = _log_sigmoid(_dot_nt(n, wf_ref[...])[:, 0:N_HEADS] + bf_ref[...])


def _proj_sample(h, mod, gain, w, wft, bf_row):
    m, d = h.shape
    return pl.pallas_call(
        _proj_sample_kernel,
        grid=(1,),
        in_specs=[_const_spec((m, d)), _const_spec((1, d)), _mod_spec(mod, m, 1, 3), _mod_spec(mod, m, 1, 4),
                  _const_spec(w.shape), _const_spec(wft.shape), _const_spec(bf_row.shape)],
        out_specs=[_const_spec((m, w.shape[1])), _const_spec((m, N_HEADS))],
        out_shape=[jax.ShapeDtypeStruct((m, w.shape[1]), F32), jax.ShapeDtypeStruct((m, N_HEADS), F32)],
        compiler_params=_params(("arbitrary",)),
        name="proj_sample",
    )(h, gain, mod, mod, w, wft, bf_row)


def _decode_kernel(pt_ref, qb_ref, *rest, fox, n_steps):
    if fox:
        (knew_ref, vnew_ref, lfnew_ref, k_hbm, v_hbm, lf_hbm, o_ref,
         kbuf, vbuf, lfbuf, zbuf, abuf, acc_ref, carry_ref, m_ref, l_ref, sem) = rest
    else:
        k_hbm, v_hbm, o_ref, kbuf, vbuf, zbuf, abuf, acc_ref, carry_ref, sem = rest
    b = pl.program_id(0)
    c = pl.program_id(1)
    step = b * n_steps + c
    total = pl.num_programs(0) * n_steps
    ch = PAGES_PER_STEP

    def copies(s, slot):
        sb = s // n_steps
        first_page = (n_steps - 1 - s % n_steps) * ch
        out = []
        for p in range(ch):
            page = pt_ref[sb, first_page + p]
            out.append(pltpu.make_async_copy(k_hbm.at[page], kbuf.at[slot, p], sem.at[0, slot]))
            out.append(pltpu.make_async_copy(v_hbm.at[page], vbuf.at[slot, p], sem.at[1, slot]))
            if fox:
                out.append(pltpu.make_async_copy(lf_hbm.at[page], lfbuf.at[slot, p], sem.at[2, slot]))
        return out

    slot = step % 2

    @pl.when(step == 0)
    def _():
        for cp in copies(step, slot):
            cp.start()

    @pl.when(step + 1 < total)
    def _():
        for cp in copies(step + 1, 1 - slot):
            cp.start()

    sel_r =lax.broadcasted_iota(jnp.int32, (N_HEADS, 2 * N_HEADS * 8), 0)
    sel_c = lax.broadcasted_iota(jnp.int32, (N_HEADS, 2 * N_HEADS * 8), 1)
    sel = jnp.where((sel_c % (N_HEADS * 8)) // 8 == sel_r, 1.0, 0.0).astype(BF)

    def scores(page_ref):
        parts = []
        for h in range(N_HEADS):
            part = page_ref[h, 0:8, :] * qb_ref[0, h, 0:8, :]
            for g in range(1, HEAD_DIM // 8):
                part = part + page_ref[h, g * 8:(g + 1) * 8, :] * qb_ref[0, h, g * 8:(g + 1) * 8, :]
            parts.append(part)
        stack = jnp.concatenate(parts, axis=0)
        hi, lo = _split2(stack)
        return _dot(sel, jnp.concatenate([hi, lo], axis=0))

    @pl.when(c == 0)
    def _():
        if fox:
            m_ref[...] = scores(knew_ref.at[0]) * (QK_SCALE * LOG2E)
            l_ref[...] = jnp.ones_like(l_ref)
            carry_ref[...] = jnp.broadcast_to(lfnew_ref[0], carry_ref.shape)
            for h in range(N_HEADS):
                acc_ref[h] = jnp.where(lax.broadcasted_iota(jnp.int32, (HEAD_DIM, LANES), 1) == 0,
                                       vnew_ref[0, h], 0.0)
        else:
            carry_ref[...] = jnp.zeros_like(carry_ref)
            acc_ref[...] = jnp.zeros_like(acc_ref)

    for cp in copies(step, slot):
        cp.wait()

    for p in range(ch):
        zbuf[p] = scores(kbuf.at[slot, p])

    carry = carry_ref[...]
    tri = _tri(LANES, lambda r, cc: r > cc)
    if fox:
        tri3 = jnp.concatenate([tri, tri, tri], axis=0)
        logits = [None] * ch
        for p in reversed(range(ch)):
            lf = lfbuf[slot, p]
            hi, mid, lo = _split3(lf)
            suffix = _dot(jnp.concatenate([hi, mid, lo], axis=1), tri3)
            logits[p] = (zbuf[p] * QK_SCALE + suffix + carry) * LOG2E
            carry = carry + jnp.sum(lf, axis=-1, keepdims=True)
        mx = logits[0]
        for p in range(1, ch):
            mx = jnp.maximum(mx, logits[p])
        m_old = m_ref[...]
        m_new = jnp.maximum(m_old, jnp.max(mx, axis=-1, keepdims=True))
        alpha = jnp.exp2(m_old - m_new)
        lsum = jnp.zeros((N_HEADS, LANES), F32)
        for p in range(ch):
            a = jnp.exp2(logits[p] - m_new)
            abuf[p] = a
            lsum = lsum + a
        l_ref[...] = alpha * l_ref[...] + jnp.sum(lsum, axis=-1, keepdims=True)
        m_ref[...] = m_new
    else:
        tri2 = jnp.concatenate([tri, tri], axis=0)
        for p in reversed(range(ch)):
            z = zbuf[p] * (QK_SCALE * LOG2E)
            sp = jnp.maximum(z, 0.0) + jnp.log2(1.0 + jnp.exp2(-jnp.abs(z)))
            hi, lo = _split2(sp)
            rest_ = _dot(jnp.concatenate([hi, lo], axis=1), tri2)
            abuf[p] = jnp.exp2(z - sp - rest_ - carry)
            carry = carry + jnp.sum(sp, axis=-1, keepdims=True)
    carry_ref[...] = carry

    for h in range(N_HEADS):
        acc = acc_ref[h]
        if fox:
            acc = acc * alpha[h:h + 1, :]
        for p in range(ch):
            acc = acc + abuf[p, h:h + 1, :] * vbuf[slot, p, h]
        acc_ref[h] = acc

    @pl.when(c == n_steps - 1)
    def _():
        for h in range(N_HEADS):
            acc = acc_ref[h]
            if fox:
                acc = acc / l_ref[h:h + 1, :]
            o_ref[0, h] = jnp.sum(acc, axis=-1, keepdims=True)


def _decode(page_table, qb, k_cache, v_cache, fox_args=None):
    n, n_pages = page_table.shape
    fox = fox_args is not None
    ch = PAGES_PER_STEP
    n_steps = n_pages // ch
    page = k_cache.shape[1:]
    seq_spec = pl.BlockSpec((1,) + qb.shape[1:], lambda b, c, pt: (b, 0, 0, 0))
    any_spec = pl.BlockSpec(memory_space=pl.ANY)
    in_specs = [seq_spec]
    args = [qb]
    scratch = [pltpu.VMEM((2, ch) + page, F32), pltpu.VMEM((2, ch) + page, F32)]
    if fox:
        knew, vnew, lfnew, lf_cache = fox_args
        in_specs += [seq_spec, seq_spec, pl.BlockSpec((1, N_HEADS, 1), lambda b, c, pt: (b, 0, 0))]
        args += [knew, vnew, lfnew]
        scratch.append(pltpu.VMEM((2, ch, N_HEADS, LANES), F32))
    in_specs += [any_spec, any_spec]
    args += [k_cache, v_cache]
    if fox:
        in_specs.append(any_spec)
        args.append(lf_cache)
    scratch += [pltpu.VMEM((ch, N_HEADS, LANES), F32), pltpu.VMEM((ch, N_HEADS, LANES), F32),
                pltpu.VMEM((N_HEADS, HEAD_DIM, LANES), F32), pltpu.VMEM((N_HEADS, LANES), F32)]
    if fox:
        scratch += [pltpu.VMEM((N_HEADS, LANES), F32), pltpu.VMEM((N_HEADS, LANES), F32)]
    scratch.append(pltpu.SemaphoreType.DMA((3, 2)))
    return pl.pallas_call(
        functools.partial(_decode_kernel, fox=fox, n_steps=n_steps),
        grid_spec=pltpu.PrefetchScalarGridSpec(
            num_scalar_prefetch=1,
            grid=(n, n_steps),
            in_specs=in_specs,
            out_specs=pl.BlockSpec((1, N_HEADS, HEAD_DIM, 1), lambda b, c, pt: (b, 0, 0, 0)),
            scratch_shapes=scratch),
        out_shape=jax.ShapeDtypeStruct((n, N_HEADS, HEAD_DIM, 1), F32),
        compiler_params=_params(("arbitrary", "arbitrary")),
        name="fox_decode" if fox else "sb_decode",
    )(page_table, *args)


def _ffn_weights(w_up, w_down):
    assert w_down.shape[0] % FF_CHUNK == 0
    return w_up.astype(BF), w_down.astype(BF)


def _lane_rep(x):
    n = x.shape[0]
    return jnp.broadcast_to(x.reshape(n, N_HEADS, HEAD_DIM, 1), (n, N_HEADS, HEAD_DIM, LANES))


def kernel(x_prompt, x_sample, c_prompt, c_sample, cache_sb_k, cache_sb_v, cache_fox_k, cache_fox_v, cache_fox_logf, page_table, norm_ffn1, norm_mix, norm_ffn2, w_ada, b_ada, w_up1, w_down1, w_up2, w_down2, w_in, b_forget, w_br_sb, w_br_fox, w_out, norm_final):
    batch, seq, d = x_prompt.shape
    n_dec = x_sample.shape[0]
    depth = w_ada.shape[0]
    assert depth == 1 and x_sample.shape[1] == 1
    assert seq % ROW_TILE == 0 and ROW_TILE % ATT_TILE == 0
    assert page_table.shape[1] % PAGES_PER_STEP == 0 and cache_sb_k.shape[2] == LANES

    wt = jnp.transpose(w_in[0])
    dh = D_HEADS
    wq = jnp.concatenate([wt[0:dh], wt[3 * dh:4 * dh]], axis=0).T.astype(BF)
    wkvt = jnp.concatenate([wt[dh:3 * dh], wt[4 * dh:6 * dh]], axis=0).astype(BF)
    wqkv_s = wt[0:6 * dh].T.astype(BF)
    wft = jnp.concatenate([wt[6 * dh:6 * dh + N_HEADS], jnp.zeros((N_HEADS, d), F32)], axis=0).astype(BF)
    wgs = wt[6 * dh + N_HEADS:6 * dh + N_HEADS + d].T.astype(BF)
    wgf = wt[6 * dh + N_HEADS + d:].T.astype(BF)
    wbs, wbf, wo = w_br_sb[0].astype(BF), w_br_fox[0].astype(BF), w_out[0].astype(BF)
    ffn1_w = _ffn_weights(w_up1[0], w_down1[0])
    ffn2_w = _ffn_weights(w_up2[0], w_down2[0])
    g1, gm, g2, gf = norm_ffn1, norm_mix, norm_ffn2, norm_final.reshape(1, d)
    bf_col = b_forget.reshape(N_HEADS, 1)
    bf_row = b_forget.reshape(1, N_HEADS)

    n_c = batch + n_dec
    n_pad = -n_c % 8
    c_all = jnp.concatenate([c_prompt, c_sample, jnp.zeros((n_pad, d), F32)], axis=0)
    mod = _adaln(c_all, w_ada[0], b_ada)
    mod_p = mod[:batch].reshape(batch, 1, N_MOD * d)
    mod_s = mod[batch:n_c].reshape(1, n_dec, N_MOD * d)

    tpb = seq // ROW_TILE
    xp = x_prompt.reshape(batch * seq, d)
    h = _ffn(xp, mod_p, g1, *ffn1_w, ROW_TILE, tpb, 0)
    q, ksb_t, vsb_t, kfx_t, vfx_t, sbb, fxk, fxv, lft, cum = _proj_prompt(h, mod_p, gm, wq, wkvt, wft, bf_col,
                                                                          batch, seq)
    cum = cum.reshape(batch, N_HEADS, seq // FOX_KEY_TILE, FOX_KEY_TILE)
    o_sb, o_fx = _prompt_attention(q, sbb, fxk, fxv, cum, batch, seq)
    y_prompt = _post(h, o_sb.reshape(batch * seq, dh), o_fx.reshape(batch * seq, dh), mod_p, gm, g2, gf,
                     wgs, wgf, wbs, wbf, wo, *ffn2_w, ROW_TILE, tpb).reshape(batch, seq, d)

    def state(t):
        return jnp.transpose(t.reshape(batch, N_HEADS, HEAD_DIM, seq), (0, 3, 1, 2))[None]

    lf_prompt = jnp.transpose(lft, (0, 2, 1))[None]

    xs = x_sample.reshape(n_dec, d)
    hs = _ffn(xs, mod_s, g1, *ffn1_w, n_dec, 1, 0)
    ps, lfs = _proj_sample(hs, mod_s, gm, wqkv_s, wft, bf_row)
    q_sb, k_sb, v_sb, q_fx, k_fx, v_fx = [ps[:, i * dh:(i + 1) * dh] for i in range(6)]
    view = lambda cache: jnp.transpose(cache[0], (0, 2, 3, 1))
    o_sb_s = _decode(page_table, _lane_rep(q_sb), view(cache_sb_k), view(cache_sb_v))
    lf_view = jnp.transpose(cache_fox_logf[0], (0, 2, 1))
    o_fx_s = _decode(page_table, _lane_rep(q_fx), view(cache_fox_k), view(cache_fox_v),
                     (_lane_rep(k_fx), _lane_rep(v_fx), lfs.reshape(n_dec, N_HEADS, 1), lf_view))
    y_sample = _post(hs, o_sb_s.reshape(n_dec, dh).astype(BF), o_fx_s.reshape(n_dec, dh).astype(BF), mod_s,
                     gm, g2, gf, wgs, wgf, wbs, wbf, wo, *ffn2_w, n_dec, 1).reshape(n_dec, 1, d)

    st = lambda x: x.reshape(1, n_dec, 1, N_HEADS, HEAD_DIM)
    return (y_prompt, y_sample, state(ksb_t), state(vsb_t), state(kfx_t), state(vfx_t), lf_prompt,
            st(k_sb), st(v_sb), st(k_fx), st(v_fx), lfs.reshape(1, n_dec, 1, N_HEADS))
```

```python
import functools

import jax
import jax.numpy as jnp
from jax import lax
from jax.experimental import pallas as pl
from jax.experimental.pallas import tpu as pltpu

HEAD_DIM = 64
N_HEADS = 8
D_HEADS = N_HEADS * HEAD_DIM
N_MOD = 9
EPS = 1e-6
LOG2E = 1.4426950408889634
QK_SCALE = HEAD_DIM ** -0.5
NEG_BIG = -1e30

LANES = 128
FF_CHUNK = 256
ROW_TILE = 512
ATT_TILE = 256
FOX_KEY_TILE = 512
PAIRS_PER_STEP = 4
PAGES_PER_STEP = 32
VMEM_LIMIT = 56 * 1024 * 1024

BF = jnp.bfloat16
F32 = jnp.float32


def _dot(a, b):
    return jnp.dot(a, b, preferred_element_type=F32)


def _dot_nt(a, b):
    return lax.dot_general(a, b, (((1,), (1,)), ((), ())), preferred_element_type=F32)


def _split2(x):
    hi = x.astype(BF)
    lo = (x - hi.astype(F32)).astype(BF)
    return hi, lo


def _split3(x):
    hi = x.astype(BF)
    r = x - hi.astype(F32)
    mid = r.astype(BF)
    lo = (r - mid.astype(F32)).astype(BF)
    return hi, mid, lo


def _sigmoid(x):
    return 1.0 / (1.0 + jnp.exp(-x))


def _rms(x, g):
    ms = jnp.mean(x * x, axis=-1, keepdims=True)
    return x * lax.rsqrt(ms + EPS) * g


def _norm_mod(x, g, shift, scale):
    return _rms(x, g) * (1.0 + scale) + shift


def _tri(n, fn):
    r = lax.broadcasted_iota(jnp.int32, (n, n), 0)
    c = lax.broadcasted_iota(jnp.int32, (n, n), 1)
    return jnp.where(fn(r, c), 1.0, 0.0).astype(BF)


def _const_spec(shape):
    nd = len(shape)
    return pl.BlockSpec(shape, lambda *_: (0,) * nd, pipeline_mode=pl.Buffered(1))


def _params(sem):
    return pltpu.CompilerParams(dimension_semantics=sem, vmem_limit_bytes=VMEM_LIMIT)


def _adaln_kernel(c_ref, w_ref, b_ref, o_ref):
    c = c_ref[...]
    s = (c * _sigmoid(c)).astype(BF)
    o_ref[...] = _dot(s, w_ref[...].astype(BF)) + b_ref[...]


def _adaln(c, w, b):
    n, d = c.shape
    cols = w.shape[1]
    blk = d
    return pl.pallas_call(
        _adaln_kernel,
        grid=(cols // blk,),
        in_specs=[pl.BlockSpec((n, d), lambda j: (0, 0)),
                  pl.BlockSpec((d, blk), lambda j: (0, j)),
                  pl.BlockSpec((1, blk), lambda j: (0, j))],
        out_specs=pl.BlockSpec((n, blk), lambda j: (0, j)),
        out_shape=jax.ShapeDtypeStruct((n, cols), F32),
        compiler_params=_params(("arbitrary",)),
        name="adaln",
    )(c, w, b)


def _swiglu_acc(xn, wup_ref, wd_ref, acc_ref):
    ff = wd_ref.shape[0]
    for c in range(ff // FF_CHUNK):
        lo = c * FF_CHUNK
        g = _dot(xn, wup_ref[:, lo:lo + FF_CHUNK])
        u = _dot(xn, wup_ref[:, ff + lo:ff + lo + FF_CHUNK])
        a = (g * _sigmoid(g) * u).astype(BF)
        part = _dot(a, wd_ref[lo:lo + FF_CHUNK, :])
        if c == 0:
            acc_ref[...] = part
        else:
            acc_ref[...] += part


def _ffn_kernel(x_ref, g_ref, sh_ref, sc_ref, gt_ref, wup_ref, wd_ref, o_ref, acc_ref):
    x = x_ref[...]
    xn = _norm_mod(x, g_ref[...], sh_ref[0], sc_ref[0]).astype(BF)
    _swiglu_acc(xn, wup_ref, wd_ref, acc_ref)
    o_ref[...] = x + 0.5 * gt_ref[0] * acc_ref[...]


def _mod_spec(mod, tm, tiles_per_group, j):
    lm = mod.shape[1]
    d = mod.shape[2] // N_MOD
    if lm == 1:
        return pl.BlockSpec((1, 1, d), lambda i: (i // tiles_per_group, 0, j))
    return pl.BlockSpec((1, tm, d), lambda i: (0, i, j))


def _ffn(x, mod, gain, wup, wd, tm, tiles_per_group, first_mod):
    m, d = x.shape
    row = pl.BlockSpec((tm, d), lambda i: (i, 0))
    return pl.pallas_call(
        _ffn_kernel,
        grid=(m // tm,),
        in_specs=[row, _const_spec((1, d)),
                  _mod_spec(mod, tm, tiles_per_group, first_mod),
                  _mod_spec(mod, tm, tiles_per_group, first_mod + 1),
                  _mod_spec(mod, tm, tiles_per_group, first_mod + 2),
                  _const_spec(wup.shape), _const_spec(wd.shape)],
        out_specs=row,
        out_shape=jax.ShapeDtypeStruct((m, d), F32),
        scratch_shapes=[pltpu.VMEM((tm, d), F32)],
        compiler_params=_params(("arbitrary",)),
        name="ffn1",
    )(x, gain, mod, mod, mod, wup, wd)


def _log_sigmoid(x):
    return jnp.minimum(x, 0.0) - jnp.log1p(jnp.exp(-jnp.abs(x)))


def _proj_prompt_kernel(h_ref, g_ref, sh_ref, sc_ref, wq_ref, wkv_ref, wf_ref, bf_ref,
                        q_ref, ksb_ref, vsb_ref, kfx_ref, vfx_ref, sbb_ref, fxk_ref, fxv_ref, lf_ref, cum_ref,
                        carry_ref, *, tiles_per_batch):
    i = pl.program_id(0)
    tm = h_ref.shape[0]
    n = _norm_mod(h_ref[...], g_ref[...], sh_ref[0], sc_ref[0]).astype(BF)
    q_ref[0] = (_dot(n, wq_ref[...]) * (QK_SCALE * LOG2E)).astype(BF)
    kv = _dot_nt(wkv_ref[...], n)
    for s, ref in enumerate((ksb_ref, vsb_ref, kfx_ref, vfx_ref)):
        ref[0] = kv[s * D_HEADS:(s + 1) * D_HEADS, :]

    def pair_tiles(ref, first_row):
        n_grp, n_blk, _, width = ref.shape[1:]
        for g in range(n_grp):
            r0 = first_row + g * LANES
            for j in range(n_blk):
                ref[0, g, j] = kv[r0:r0 + LANES, j * width:(j + 1) * width].astype(BF)

    pair_tiles(sbb_ref, 0)
    pair_tiles(fxk_ref, 2 * D_HEADS)
    n_blk, _, width = fxv_ref.shape[2:]
    one_row = jnp.where(lax.broadcasted_iota(jnp.int32, (8, width), 0) == 0, 1.0, 0.0).astype(BF)
    other = jnp.concatenate([one_row, jnp.zeros((HEAD_DIM - 8, width), BF)], axis=0)
    for h in range(N_HEADS):
        r0 = 3 * D_HEADS + h * HEAD_DIM
        for j in range(n_blk):
            own = kv[r0:r0 + HEAD_DIM, j * width:(j + 1) * width].astype(BF)
            fxv_ref[0, h, j] = jnp.concatenate([own, other] if h % 2 == 0 else [other, own], axis=0)
    lf = _log_sigmoid(_dot_nt(wf_ref[...], n)[0:N_HEADS, :] + bf_ref[...])
    lf_ref[0] = lf

    @pl.when(i % tiles_per_batch == 0)
    def _():
        carry_ref[...] = jnp.zeros_like(carry_ref)

    tri = _tri(LANES, lambda r, c: r <= c)
    carry = carry_ref[...]
    for c in range(tm // LANES):
        hi, mid, lo = _split3(lf[:, c * LANES:(c + 1) * LANES])
        cs = _dot(hi, tri) + _dot(mid, tri) + _dot(lo, tri) + carry
        cum_ref[0, :, c * LANES:(c + 1) * LANES] = cs
        carry = jnp.broadcast_to(cs[:, LANES - 1:LANES], carry.shape)
    carry_ref[...] = carry


def _proj_prompt(h, mod, gain, wq, wkvt, wft, bf, batch, seq):
    m, d = h.shape
    tm = ROW_TILE
    tpb = seq // tm
    n_pair = D_HEADS // LANES
    t_spec = lambda rows: pl.BlockSpec((1, rows, tm), lambda i: (i // tpb, 0, i % tpb))
    t_shape = lambda rows: jax.ShapeDtypeStruct((batch, rows, seq), F32)
    b_spec = lambda n, w: pl.BlockSpec((1, n, tm // w, LANES, w), lambda i: (i // tpb, 0, i % tpb, 0, 0))
    b_shape = lambda n, w: jax.ShapeDtypeStruct((batch, n, seq // w, LANES, w), BF)
    tk = FOX_KEY_TILE
    return pl.pallas_call(
        functools.partial(_proj_prompt_kernel, tiles_per_batch=tpb),
        grid=(m // tm,),
        in_specs=[pl.BlockSpec((tm, d), lambda i: (i, 0)), _const_spec((1, d)),
                  _mod_spec(mod, tm, tpb, 3), _mod_spec(mod, tm, tpb, 4),
                  _const_spec(wq.shape), _const_spec(wkvt.shape), _const_spec(wft.shape),
                  _const_spec(bf.shape)],
        out_specs=[pl.BlockSpec((1, tm, wq.shape[1]), lambda i: (i // tpb, i % tpb, 0)),
                   t_spec(D_HEADS), t_spec(D_HEADS), t_spec(D_HEADS), t_spec(D_HEADS),
                   b_spec(2 * n_pair, ATT_TILE), b_spec(n_pair, tk), b_spec(N_HEADS, tk),
                   t_spec(N_HEADS), t_spec(N_HEADS)],
        out_shape=[jax.ShapeDtypeStruct((batch, seq, wq.shape[1]), BF),
                   t_shape(D_HEADS), t_shape(D_HEADS), t_shape(D_HEADS), t_shape(D_HEADS),
                   b_shape(2 * n_pair, ATT_TILE), b_shape(n_pair, tk), b_shape(N_HEADS, tk),
                   t_shape(N_HEADS), t_shape(N_HEADS)],
        scratch_shapes=[pltpu.VMEM((N_HEADS, LANES), F32)],
        compiler_params=_params(("arbitrary",)),
        name="proj_prompt",
    )(h, gain, mod, mod, wq, wkvt, wft, bf)


def _head_masks():
    lane = lax.broadcasted_iota(jnp.int32, (ATT_TILE, LANES), 1)
    return lane < HEAD_DIM


def _head_queries(q_ref):
    first = _head_masks()
    qs = []
    for pair in range(q_ref.shape[2] // LANES):
        q = q_ref[0, :, pair * LANES:(pair + 1) * LANES]
        zero = jnp.zeros_like(q)
        qs += [jnp.where(first, q, zero), jnp.where(first, zero, q)]
    return first, qs


def _skewed(n, stages):
    depth = len(stages)
    for step in range(n + depth - 1):
        for k in reversed(range(depth)):
            if 0 <= step - k < n:
                stages[k](step - k)


def _store_heads(o_ref, first, outs):
    for pair in range(len(outs) // 2):
        o_ref[0, :, pair * LANES:(pair + 1) * LANES] = jnp.where(first, outs[2 * pair], outs[2 * pair + 1]).astype(BF)


def _sb_prompt_kernel(q_ref, k_ref, v_ref, o_ref):
    qi = pl.program_id(2)
    t = ATT_TILE
    first, qs = _head_queries(q_ref)
    n_h = len(qs)
    tri = _tri(t, lambda r, c: r > c)
    row = lax.broadcasted_iota(jnp.int32, (t, t), 0)
    col = lax.broadcasted_iota(jnp.int32, (t, t), 1)
    causal = col < row

    heads = range(n_h)

    def blocks(kbs, carries, accs, mask):
        carries, accs = list(carries), list(accs)
        items = [(kb, h) for kb in kbs for h in heads]
        zs = [_dot(qs[h], k_ref[0, h // 2, kb]) for kb, h in items]
        sps = [jnp.maximum(z, 0.0) + jnp.log2(1.0 + jnp.exp2(-jnp.abs(z))) for z in zs]
        logsig = [z - sp for z, sp in zip(zs, sps)]
        if mask is not None:
            sps = [jnp.where(mask, sp, 0.0) for sp in sps]
        rests = [_dot(sp.astype(BF), tri) for sp in sps]
        totals = [jnp.sum(sp, axis=-1, keepdims=True) for sp in sps]
        pvs = []
        for i, (kb, h) in enumerate(items):
            a = jnp.exp2(logsig[i] - rests[i] - carries[h])
            if mask is not None:
                a = jnp.where(mask, a, 0.0)
            pvs.append(_dot_nt(a.astype(BF), v_ref[0, h // 2, kb]))
            carries[h] = carries[h] + totals[i]
        for i, (kb, h) in enumerate(items):
            accs[h] = accs[h] + pvs[i]
        return carries, accs

    carries, accs = blocks([qi], [jnp.zeros((t, 1), F32)] * n_h, [jnp.zeros((t, LANES), F32)] * n_h, causal)
    state = tuple(carries) + tuple(accs)

    def step(kbs, st):
        carries, accs = blocks(kbs, st[:n_h], st[n_h:], None)
        return tuple(carries) + tuple(accs)

    state = lax.cond(qi % 2 == 1, lambda st: step([qi - 1], st), lambda st: st, state)
    top = qi - 1 - qi % 2
    state = lax.cond(qi % 4 >= 2, lambda st: step([top, top - 1], st), lambda st: st, state)
    top = qi - 1 - qi % 4
    state = lax.fori_loop(0, qi // 4, lambda j, st: step([top - 4 * j - i for i in range(4)], st), state)
    _store_heads(o_ref, first, list(state[n_h:]))


def _fox_prompt_kernel(q_ref, k_ref, v_ref, c_ref, o_ref):
    qi = pl.program_id(2)
    t = ATT_TILE
    tk = k_ref.shape[-1]
    per = tk // t
    kd = qi // per
    first, qs = _head_queries(q_ref)
    n_h = len(qs)
    row = lax.broadcasted_iota(jnp.int32, (t, tk), 0)
    col = lax.broadcasted_iota(jnp.int32, (t, tk), 1)
    causal = col <= row + (qi % per) * t
    bases = [c_ref[0, h, pl.ds(kd, 1), :][:, 0:1] for h in range(n_h)]

    def blocks(kbs, ms, accs, mask):
        ms, accs = list(ms), list(accs)
        items = [(kb, h) for kb in kbs for h in range(n_h)]
        s, mrow, alpha, pv = {}, {}, {}, {}

        def logits(i):
            kb, h = items[i]
            s[i] = _dot(qs[h], k_ref[0, h // 2, kb]) - (c_ref[0, h, pl.ds(kb, 1), :] - bases[h]) * LOG2E
            if mask is not None:
                s[i] = jnp.where(mask, s[i], NEG_BIG)

        def row_max(i):
            h = items[i][1]
            mrow[i] = jnp.maximum(ms[h], jnp.max(s[i], axis=-1, keepdims=True))
            alpha[i] = jnp.exp2(ms[h] - mrow[i])
            ms[h] = mrow[i]

        def weights(i):
            kb, h = items[i]
            pv[i] = _dot_nt(jnp.exp2(s[i] - mrow[i]).astype(BF), v_ref[0, h, kb])

        def accumulate(i):
            h = items[i][1]
            accs[h] = alpha[i] * accs[h] + pv[i]

        _skewed(len(items), (logits, row_max, weights, accumulate))
        return tuple(ms) + tuple(accs)

    state = blocks([kd], [jnp.full((t, 1), NEG_BIG, F32)] * n_h, [jnp.zeros((t, LANES), F32)] * n_h, causal)
    state = lax.cond(kd % 2 == 1, lambda st: blocks([kd - 1], st[:n_h], st[n_h:], None), lambda st: st, state)
    first_pair = kd - 1 - kd % 2
    state = lax.fori_loop(
        0, kd // 2,
        lambda j, st: blocks([first_pair - 2 * j, first_pair - 2 * j - 1], st[:n_h], st[n_h:], None), state)
    outs = []
    for h in range(n_h):
        acc = state[n_h + h]
        denom_lane = HEAD_DIM if h % 2 == 0 else 0
        outs.append(acc / acc[:, denom_lane:denom_lane + 1])
    _store_heads(o_ref, first, outs)


def _prompt_attention(q, sbb, fxk, fxv, cum, batch, seq):
    t = ATT_TILE
    pps = PAIRS_PER_STEP
    n_pair = D_HEADS // LANES
    n_grp = n_pair // pps
    grid = (batch, n_grp, seq // t)
    sem = ("arbitrary", "arbitrary", "arbitrary")
    width = pps * LANES

    def kv_spec(blocked, first_block, per_step=pps):
        return pl.BlockSpec((1, per_step) + blocked.shape[2:], lambda b, g, qi: (b, first_block + g, 0, 0, 0),
                            pipeline_mode=pl.Buffered(1))

    def q_spec(first_block):
        return pl.BlockSpec((1, t, width), lambda b, g, qi: (b, qi, first_block + g))

    out_spec = pl.BlockSpec((1, t, width), lambda b, g, qi: (b, qi, g))
    out_shape = jax.ShapeDtypeStruct((batch, seq, D_HEADS), BF)
    o_sb = pl.pallas_call(
        _sb_prompt_kernel,
        grid=grid,
        in_specs=[q_spec(0), kv_spec(sbb, 0), kv_spec(sbb, n_grp)],
        out_specs=out_spec,
        out_shape=out_shape,
        compiler_params=_params(sem),
        name="sb_prompt",
    )(q, sbb, sbb)
    o_fx = pl.pallas_call(
        _fox_prompt_kernel,
        grid=grid,
        in_specs=[q_spec(n_grp), kv_spec(fxk, 0), kv_spec(fxv, 0, 2 * pps),
                  pl.BlockSpec((1, 2 * pps) + cum.shape[2:], lambda b, g, qi: (b, g, 0, 0))],
        out_specs=out_spec,
        out_shape=out_shape,
        compiler_params=_params(sem),
        name="fox_prompt",
    )(q, fxk, fxv, cum)
    return o_sb, o_fx


def _post_kernel(h_ref, osb_ref, ofx_ref, gmix_ref, sh2_ref, sc2_ref, gt2_ref, gffn_ref, sh3_ref, sc3_ref,
                 gt3_ref, gfin_ref, wgs_ref, wgf_ref, wbs_ref, wbf_ref, wo_ref, wup_ref, wd_ref,
                 y_ref, acc_ref):
    h = h_ref[...]
    n = _norm_mod(h, gmix_ref[...], sh2_ref[0], sc2_ref[0]).astype(BF)
    merged = (_sigmoid(_dot(n, wgs_ref[...])) * _dot(osb_ref[...], wbs_ref[...])
              + _sigmoid(_dot(n, wgf_ref[...])) * _dot(ofx_ref[...], wbf_ref[...]))
    h2 = h + gt2_ref[0] * _dot(merged.astype(BF), wo_ref[...])
    xn = _norm_mod(h2, gffn_ref[...], sh3_ref[0], sc3_ref[0]).astype(BF)
    _swiglu_acc(xn, wup_ref, wd_ref, acc_ref)
    h3 = h2 + 0.5 * gt3_ref[0] * acc_ref[...]
    y_ref[...] = _rms(h3, gfin_ref[...])


def _post(h, o_sb, o_fx, mod, gmix, gffn, gfin, wgs, wgf, wbs, wbf, wo, wup, wd, tm, tiles_per_group):
    m, d = h.shape
    row = pl.BlockSpec((tm, d), lambda i: (i, 0))
    orow = pl.BlockSpec((tm, o_sb.shape[1]), lambda i: (i, 0))
    ms = lambda j: _mod_spec(mod, tm, tiles_per_group, j)
    return pl.pallas_call(
        _post_kernel,
        grid=(m // tm,),
        in_specs=[row, orow, orow, _const_spec((1, d)), ms(3), ms(4), ms(5), _const_spec((1, d)), ms(6), ms(7),
                  ms(8), _const_spec((1, d)),
                  _const_spec(wgs.shape), _const_spec(wgf.shape), _const_spec(wbs.shape), _const_spec(wbf.shape),
                  _const_spec(wo.shape), _const_spec(wup.shape), _const_spec(wd.shape)],
        out_specs=row,
        out_shape=jax.ShapeDtypeStruct((m, d), F32),
        scratch_shapes=[pltpu.VMEM((tm, d), F32)],
        compiler_params=_params(("arbitrary",)),
        name="post",
    )(h, o_sb, o_fx, gmix, mod, mod, mod, gffn, mod, mod, mod, gfin, wgs, wgf, wbs, wbf, wo, wup, wd)


def _proj_sample_kernel(h_ref, g_ref, sh_ref, sc_ref, w_ref, wf_ref, bf_ref, p_ref, lf_ref):
    n = _norm_mod(h_ref[...], g_ref[...], sh_ref[0], sc_ref[0]).astype(BF)
    p_ref[...] = _dot(n, w_ref[...])
    lf_ref[...] = _log_sigmoid(_dot_nt(n, wf_ref[...])[:, 0:N_HEADS] + bf_ref[...])


def _proj_sample(h, mod, gain, w, wft, bf_row):
    m, d = h.shape
    return pl.pallas_call(
        _proj_sample_kernel,
        grid=(1,),
        in_specs=[_const_spec((m, d)), _const_spec((1, d)), _mod_spec(mod, m, 1, 3), _mod_spec(mod, m, 1, 4),
                  _const_spec(w.shape), _const_spec(wft.shape), _const_spec(bf_row.shape)],
        out_specs=[_const_spec((m, w.shape[1])), _const_spec((m, N_HEADS))],
        out_shape=[jax.ShapeDtypeStruct((m, w.shape[1]), F32), jax.ShapeDtypeStruct((m, N_HEADS), F32)],
        compiler_params=_params(("arbitrary",)),
        name="proj_sample",
    )(h, gain, mod, mod, w, wft, bf_row)


def _decode_kernel(pt_ref, qb_ref, *rest, fox, n_steps):
    if fox:
        (knew_ref, vnew_ref, lfnew_ref, k_hbm, v_hbm, lf_hbm, o_ref,
         kbuf, vbuf, lfbuf, zbuf, abuf, acc_ref, carry_ref, m_ref, l_ref, sem) = rest
    else:
        k_hbm, v_hbm, o_ref, kbuf, vbuf, zbuf, abuf, acc_ref, carry_ref, sem = rest
    b = pl.program_id(0)
    c = pl.program_id(1)
    step = b * n_steps + c
    total = pl.num_programs(0) * n_steps
    ch = PAGES_PER_STEP

    def copies(s, slot):
        sb = s // n_steps
        first_page = (n_steps - 1 - s % n_steps) * ch
        out = []
        for p in range(ch):
            page = pt_ref[sb, first_page + p]
            out.append(pltpu.make_async_copy(k_hbm.at[page], kbuf.at[slot, p], sem.at[0, slot]))
            out.append(pltpu.make_async_copy(v_hbm.at[page], vbuf.at[slot, p], sem.at[1, slot]))
            if fox:
                out.append(pltpu.make_async_copy(lf_hbm.at[page], lfbuf.at[slot, p], sem.at[2, slot]))
        return out

    slot = step % 2

    @pl.when(step == 0)
    def _():
        for cp in copies(step, slot):
            cp.start()

    @pl.when(step + 1 < total)
    def _():
        for cp in copies(step + 1, 1 - slot):
            cp.start()

    sel_r =lax.broadcasted_iota(jnp.int32, (N_HEADS, 2 * N_HEADS * 8), 0)
    sel_c = lax.broadcasted_iota(jnp.int32, (N_HEADS, 2 * N_HEADS * 8), 1)
    sel = jnp.where((sel_c % (N_HEADS * 8)) // 8 == sel_r, 1.0, 0.0).astype(BF)

    def scores(page_ref):
        parts = []
        for h in range(N_HEADS):
            part = page_ref[h, 0:8, :] * qb_ref[0, h, 0:8, :]
            for g in range(1, HEAD_DIM // 8):
                part = part + page_ref[h, g * 8:(g + 1) * 8, :] * qb_ref[0, h, g * 8:(g + 1) * 8, :]
            parts.append(part)
        stack = jnp.concatenate(parts, axis=0)
        hi, lo = _split2(stack)
        return _dot(sel, jnp.concatenate([hi, lo], axis=0))

    @pl.when(c == 0)
    def _():
        if fox:
            m_ref[...] = scores(knew_ref.at[0]) * (QK_SCALE * LOG2E)
            l_ref[...] = jnp.ones_like(l_ref)
            carry_ref[...] = jnp.broadcast_to(lfnew_ref[0], carry_ref.shape)
            for h in range(N_HEADS):
                acc_ref[h] = jnp.where(lax.broadcasted_iota(jnp.int32, (HEAD_DIM, LANES), 1) == 0,
                                       vnew_ref[0, h], 0.0)
        else:
            carry_ref[...] = jnp.zeros_like(carry_ref)
            acc_ref[...] = jnp.zeros_like(acc_ref)

    for cp in copies(step, slot):
        cp.wait()

    for p in range(ch):
        zbuf[p] = scores(kbuf.at[slot, p])

    carry = carry_ref[...]
    tri = _tri(LANES, lambda r, cc: r > cc)
    if fox:
        tri3 = jnp.concatenate([tri, tri, tri], axis=0)
        logits = [None] * ch
        for p in reversed(range(ch)):
            lf = lfbuf[slot, p]
            hi, mid, lo = _split3(lf)
            suffix = _dot(jnp.concatenate([hi, mid, lo], axis=1), tri3)
            logits[p] = (zbuf[p] * QK_SCALE + suffix + carry) * LOG2E
            carry = carry + jnp.sum(lf, axis=-1, keepdims=True)
        mx = logits[0]
        for p in range(1, ch):
            mx = jnp.maximum(mx, logits[p])
        m_old = m_ref[...]
        m_new = jnp.maximum(m_old, jnp.max(mx, axis=-1, keepdims=True))
        alpha = jnp.exp2(m_old - m_new)
        lsum = jnp.zeros((N_HEADS, LANES), F32)
        for p in range(ch):
            a = jnp.exp2(logits[p] - m_new)
            abuf[p] = a
            lsum = lsum + a
        l_ref[...] = alpha * l_ref[...] + jnp.sum(lsum, axis=-1, keepdims=True)
        m_ref[...] = m_new
    else:
        tri2 = jnp.concatenate([tri, tri], axis=0)
        for p in reversed(range(ch)):
            z = zbuf[p] * (QK_SCALE * LOG2E)
            sp = jnp.maximum(z, 0.0) + jnp.log2(1.0 + jnp.exp2(-jnp.abs(z)))
            hi, lo = _split2(sp)
            rest_ = _dot(jnp.concatenate([hi, lo], axis=1), tri2)
            abuf[p] = jnp.exp2(z - sp - rest_ - carry)
            carry = carry + jnp.sum(sp, axis=-1, keepdims=True)
    carry_ref[...] = carry

    for h in range(N_HEADS):
        acc = acc_ref[h]
        if fox:
            acc = acc * alpha[h:h + 1, :]
        for p in range(ch):
            acc = acc + abuf[p, h:h + 1, :] * vbuf[slot, p, h]
        acc_ref[h] = acc

    @pl.when(c == n_steps - 1)
    def _():
        for h in range(N_HEADS):
            acc = acc_ref[h]
            if fox:
                acc = acc / l_ref[h:h + 1, :]
            o_ref[0, h] = jnp.sum(acc, axis=-1, keepdims=True)


def _decode(page_table, qb, k_cache, v_cache, fox_args=None):
    n, n_pages = page_table.shape
    fox = fox_args is not None
    ch = PAGES_PER_STEP
    n_steps = n_pages // ch
    page = k_cache.shape[1:]
    seq_spec = pl.BlockSpec((1,) + qb.shape[1:], lambda b, c, pt: (b, 0, 0, 0))
    any_spec = pl.BlockSpec(memory_space=pl.ANY)
    in_specs = [seq_spec]
    args = [qb]
    scratch = [pltpu.VMEM((2, ch) + page, F32), pltpu.VMEM((2, ch) + page, F32)]
    if fox:
        knew, vnew, lfnew, lf_cache = fox_args
        in_specs += [seq_spec, seq_spec, pl.BlockSpec((1, N_HEADS, 1), lambda b, c, pt: (b, 0, 0))]
        args += [knew, vnew, lfnew]
        scratch.append(pltpu.VMEM((2, ch, N_HEADS, LANES), F32))
    in_specs += [any_spec, any_spec]
    args += [k_cache, v_cache]
    if fox:
        in_specs.append(any_spec)
        args.append(lf_cache)
    scratch += [pltpu.VMEM((ch, N_HEADS, LANES), F32), pltpu.VMEM((ch, N_HEADS, LANES), F32),
                pltpu.VMEM((N_HEADS, HEAD_DIM, LANES), F32), pltpu.VMEM((N_HEADS, LANES), F32)]
    if fox:
        scratch += [pltpu.VMEM((N_HEADS, LANES), F32), pltpu.VMEM((N_HEADS, LANES), F32)]
    scratch.append(pltpu.SemaphoreType.DMA((3, 2)))
    return pl.pallas_call(
        functools.partial(_decode_kernel, fox=fox, n_steps=n_steps),
        grid_spec=pltpu.PrefetchScalarGridSpec(
            num_scalar_prefetch=1,
            grid=(n, n_steps),
            in_specs=in_specs,
            out_specs=pl.BlockSpec((1, N_HEADS, HEAD_DIM, 1), lambda b, c, pt: (b, 0, 0, 0)),
            scratch_shapes=scratch),
        out_shape=jax.ShapeDtypeStruct((n, N_HEADS, HEAD_DIM, 1), F32),
        compiler_params=_params(("arbitrary", "arbitrary")),
        name="fox_decode" if fox else "sb_decode",
    )(page_table, *args)


def _ffn_weights(w_up, w_down):
    assert w_down.shape[0] % FF_CHUNK == 0
    return w_up.astype(BF), w_down.astype(BF)


def _lane_rep(x):
    n = x.shape[0]
    return jnp.broadcast_to(x.reshape(n, N_HEADS, HEAD_DIM, 1), (n, N_HEADS, HEAD_DIM, LANES))


def kernel(x_prompt, x_sample, c_prompt, c_sample, cache_sb_k, cache_sb_v, cache_fox_k, cache_fox_v, cache_fox_logf, page_table, norm_ffn1, norm_mix, norm_ffn2, w_ada, b_ada, w_up1, w_down1, w_up2, w_down2, w_in, b_forget, w_br_sb, w_br_fox, w_out, norm_final):
    batch, seq, d = x_prompt.shape
    n_dec = x_sample.shape[0]
    depth = w_ada.shape[0]
    assert depth == 1 and x_sample.shape[1] == 1
    assert seq % ROW_TILE == 0 and ROW_TILE % ATT_TILE == 0
    assert page_table.shape[1] % PAGES_PER_STEP == 0 and cache_sb_k.shape[2] == LANES

    wt = jnp.transpose(w_in[0])
    dh = D_HEADS
    wq = jnp.concatenate([wt[0:dh], wt[3 * dh:4 * dh]], axis=0).T.astype(BF)
    wkvt = jnp.concatenate([wt[dh:3 * dh], wt[4 * dh:6 * dh]], axis=0).astype(BF)
    wqkv_s = wt[0:6 * dh].T.astype(BF)
    wft = jnp.concatenate([wt[6 * dh:6 * dh + N_HEADS], jnp.zeros((N_HEADS, d), F32)], axis=0).astype(BF)
    wgs = wt[6 * dh + N_HEADS:6 * dh + N_HEADS + d].T.astype(BF)
    wgf = wt[6 * dh + N_HEADS + d:].T.astype(BF)
    wbs, wbf, wo = w_br_sb[0].astype(BF), w_br_fox[0].astype(BF), w_out[0].astype(BF)
    ffn1_w = _ffn_weights(w_up1[0], w_down1[0])
    ffn2_w = _ffn_weights(w_up2[0], w_down2[0])
    g1, gm, g2, gf = norm_ffn1, norm_mix, norm_ffn2, norm_final.reshape(1, d)
    bf_col = b_forget.reshape(N_HEADS, 1)
    bf_row = b_forget.reshape(1, N_HEADS)

    n_c = batch + n_dec
    n_pad = -n_c % 8
    c_all = jnp.concatenate([c_prompt, c_sample, jnp.zeros((n_pad, d), F32)], axis=0)
    mod = _adaln(c_all, w_ada[0], b_ada)
    mod_p = mod[:batch].reshape(batch, 1, N_MOD * d)
    mod_s = mod[batch:n_c].reshape(1, n_dec, N_MOD * d)

    tpb = seq // ROW_TILE
    xp = x_prompt.reshape(batch * seq, d)
    h = _ffn(xp, mod_p, g1, *ffn1_w, ROW_TILE, tpb, 0)
    q, ksb_t, vsb_t, kfx_t, vfx_t, sbb, fxk, fxv, lft, cum = _proj_prompt(h, mod_p, gm, wq, wkvt, wft, bf_col,
                                                                          batch, seq)
    cum = cum.reshape(batch, N_HEADS, seq // FOX_KEY_TILE, FOX_KEY_TILE)
    o_sb, o_fx = _prompt_attention(q, sbb, fxk, fxv, cum, batch, seq)
    y_prompt = _post(h, o_sb.reshape(batch * seq, dh), o_fx.reshape(batch * seq, dh), mod_p, gm, g2, gf,
                     wgs, wgf, wbs, wbf, wo, *ffn2_w, ROW_TILE, tpb).reshape(batch, seq, d)

    def state(t):
        return jnp.transpose(t.reshape(batch, N_HEADS, HEAD_DIM, seq), (0, 3, 1, 2))[None]

    lf_prompt = jnp.transpose(lft, (0, 2, 1))[None]

    xs = x_sample.reshape(n_dec, d)
    hs = _ffn(xs, mod_s, g1, *ffn1_w, n_dec, 1, 0)
    ps, lfs = _proj_sample(hs, mod_s, gm, wqkv_s, wft, bf_row)
    q_sb, k_sb, v_sb, q_fx, k_fx, v_fx = [ps[:, i * dh:(i + 1) * dh] for i in range(6)]
    view = lambda cache: jnp.transpose(cache[0], (0, 2, 3, 1))
    o_sb_s = _decode(page_table, _lane_rep(q_sb), view(cache_sb_k), view(cache_sb_v))
    lf_view = jnp.transpose(cache_fox_logf[0], (0, 2, 1))
    o_fx_s = _decode(page_table, _lane_rep(q_fx), view(cache_fox_k), view(cache_fox_v),
                     (_lane_rep(k_fx), _lane_rep(v_fx), lfs.reshape(n_dec, N_HEADS, 1), lf_view))
    y_sample = _post(hs, o_sb_s.reshape(n_dec, dh).astype(BF), o_fx_s.reshape(n_dec, dh).astype(BF), mod_s,
                     gm, g2, gf, wgs, wgf, wbs, wbf, wo, *ffn2_w, n_dec, 1).reshape(n_dec, 1, d)

    st = lambda x: x.reshape(1, n_dec, 1, N_HEADS, HEAD_DIM)
    return (y_prompt, y_sample, state(ksb_t), state(vsb_t), state(kfx_t), state(vfx_t), lf_prompt,
            st(k_sb), st(v_sb), st(k_fx), st(v_fx), lfs.reshape(1, n_dec, 1, N_HEADS))
```

```python
import functools

import jax
import jax.numpy as jnp
from jax import lax
from jax.experimental import pallas as pl
from jax.experimental.pallas import tpu as pltpu

HEAD_DIM = 64
N_HEADS = 8
D_HEADS = N_HEADS * HEAD_DIM
N_MOD = 9
EPS = 1e-6
LOG2E = 1.4426950408889634
QK_SCALE = HEAD_DIM ** -0.5
NEG_BIG = -1e30

LANES = 128
FF_CHUNK = 256
ROW_TILE = 512
ATT_TILE = 256
FOX_KEY_TILE = 512
PAIRS_PER_STEP = 4
PAGES_PER_STEP = 32
VMEM_LIMIT = 56 * 1024 * 1024

BF = jnp.bfloat16
F32 = jnp.float32


def _dot(a, b):
    return jnp.dot(a, b, preferred_element_type=F32)


def _dot_nt(a, b):
    return lax.dot_general(a, b, (((1,), (1,)), ((), ())), preferred_element_type=F32)


def _split2(x):
    hi = x.astype(BF)
    lo = (x - hi.astype(F32)).astype(BF)
    return hi, lo


def _split3(x):
    hi = x.astype(BF)
    r = x - hi.astype(F32)
    mid = r.astype(BF)
    lo = (r - mid.astype(F32)).astype(BF)
    return hi, mid, lo


def _sigmoid(x):
    return 1.0 / (1.0 + jnp.exp(-x))


def _rms(x, g):
    ms = jnp.mean(x * x, axis=-1, keepdims=True)
    return x * lax.rsqrt(ms + EPS) * g


def _norm_mod(x, g, shift, scale):
    return _rms(x, g) * (1.0 + scale) + shift


def _tri(n, fn):
    r = lax.broadcasted_iota(jnp.int32, (n, n), 0)
    c = lax.broadcasted_iota(jnp.int32, (n, n), 1)
    return jnp.where(fn(r, c), 1.0, 0.0).astype(BF)


def _const_spec(shape):
    nd = len(shape)
    return pl.BlockSpec(shape, lambda *_: (0,) * nd, pipeline_mode=pl.Buffered(1))


def _params(sem):
    return pltpu.CompilerParams(dimension_semantics=sem, vmem_limit_bytes=VMEM_LIMIT)


def _adaln_kernel(c_ref, w_ref, b_ref, o_ref):
    c = c_ref[...]
    s = (c * _sigmoid(c)).astype(BF)
    o_ref[...] = _dot(s, w_ref[...].astype(BF)) + b_ref[...]


def _adaln(c, w, b):
    n, d = c.shape
    cols = w.shape[1]
    blk = d
    return pl.pallas_call(
        _adaln_kernel,
        grid=(cols // blk,),
        in_specs=[pl.BlockSpec((n, d), lambda j: (0, 0)),
                  pl.BlockSpec((d, blk), lambda j: (0, j)),
                  pl.BlockSpec((1, blk), lambda j: (0, j))],
        out_specs=pl.BlockSpec((n, blk), lambda j: (0, j)),
        out_shape=jax.ShapeDtypeStruct((n, cols), F32),
        compiler_params=_params(("arbitrary",)),
        name="adaln",
    )(c, w, b)


def _swiglu_acc(xn, wup_ref, wd_ref, acc_ref):
    ff = wd_ref.shape[0]
    for c in range(ff // FF_CHUNK):
        lo = c * FF_CHUNK
        g = _dot(xn, wup_ref[:, lo:lo + FF_CHUNK])
        u = _dot(xn, wup_ref[:, ff + lo:ff + lo + FF_CHUNK])
        a = (g * _sigmoid(g) * u).astype(BF)
        part = _dot(a, wd_ref[lo:lo + FF_CHUNK, :])
        if c == 0:
            acc_ref[...] = part
        else:
            acc_ref[...] += part


def _ffn_kernel(x_ref, g_ref, sh_ref, sc_ref, gt_ref, wup_ref, wd_ref, o_ref, acc_ref):
    x = x_ref[...]
    xn = _norm_mod(x, g_ref[...], sh_ref[0], sc_ref[0]).astype(BF)
    _swiglu_acc(xn, wup_ref, wd_ref, acc_ref)
    o_ref[...] = x + 0.5 * gt_ref[0] * acc_ref[...]


def _mod_spec(mod, tm, tiles_per_group, j):
    lm = mod.shape[1]
    d = mod.shape[2] // N_MOD
    if lm == 1:
        return pl.BlockSpec((1, 1, d), lambda i: (i // tiles_per_group, 0, j))
    return pl.BlockSpec((1, tm, d), lambda i: (0, i, j))


def _ffn(x, mod, gain, wup, wd, tm, tiles_per_group, first_mod):
    m, d = x.shape
    row = pl.BlockSpec((tm, d), lambda i: (i, 0))
    return pl.pallas_call(
        _ffn_kernel,
        grid=(m // tm,),
        in_specs=[row, _const_spec((1, d)),
                  _mod_spec(mod, tm, tiles_per_group, first_mod),
                  _mod_spec(mod, tm, tiles_per_group, first_mod + 1),
                  _mod_spec(mod, tm, tiles_per_group, first_mod + 2),
                  _const_spec(wup.shape), _const_spec(wd.shape)],
        out_specs=row,
        out_shape=jax.ShapeDtypeStruct((m, d), F32),
        scratch_shapes=[pltpu.VMEM((tm, d), F32)],
        compiler_params=_params(("arbitrary",)),
        name="ffn1",
    )(x, gain, mod, mod, mod, wup, wd)


def _log_sigmoid(x):
    return jnp.minimum(x, 0.0) - jnp.log1p(jnp.exp(-jnp.abs(x)))


def _proj_prompt_kernel(h_ref, g_ref, sh_ref, sc_ref, wq_ref, wkv_ref, wf_ref, bf_ref,
                        q_ref, ksb_ref, vsb_ref, kfx_ref, vfx_ref, sbb_ref, fxk_ref, fxv_ref, lf_ref, cum_ref,
                        carry_ref, *, tiles_per_batch):
    i = pl.program_id(0)
    tm = h_ref.shape[0]
    n = _norm_mod(h_ref[...], g_ref[...], sh_ref[0], sc_ref[0]).astype(BF)
    q_ref[0] = (_dot(n, wq_ref[...]) * (QK_SCALE * LOG2E)).astype(BF)
    kv = _dot_nt(wkv_ref[...], n)
    for s, ref in enumerate((ksb_ref, vsb_ref, kfx_ref, vfx_ref)):
        ref[0] = kv[s * D_HEADS:(s + 1) * D_HEADS, :]

    def pair_tiles(ref, first_row):
        n_grp, n_blk, _, width = ref.shape[1:]
        for g in range(n_grp):
            r0 = first_row + g * LANES
            for j in range(n_blk):
                ref[0, g, j] = kv[r0:r0 + LANES, j * width:(j + 1) * width].astype(BF)

    pair_tiles(sbb_ref, 0)
    pair_tiles(fxk_ref, 2 * D_HEADS)
    n_blk, _, width = fxv_ref.shape[2:]
    one_row = jnp.where(lax.broadcasted_iota(jnp.int32, (8, width), 0) == 0, 1.0, 0.0).astype(BF)
    other = jnp.concatenate([one_row, jnp.zeros((HEAD_DIM - 8, width), BF)], axis=0)
    for h in range(N_HEADS):
        r0 = 3 * D_HEADS + h * HEAD_DIM
        for j in range(n_blk):
            own = kv[r0:r0 + HEAD_DIM, j * width:(j + 1) * width].astype(BF)
            fxv_ref[0, h, j] = jnp.concatenate([own, other] if h % 2 == 0 else [other, own], axis=0)
    lf = _log_sigmoid(_dot_nt(wf_ref[...], n)[0:N_HEADS, :] + bf_ref[...])
    lf_ref[0] = lf

    @pl.when(i % tiles_per_batch == 0)
    def _():
        carry_ref[...] = jnp.zeros_like(carry_ref)

    tri = _tri(LANES, lambda r, c: r <= c)
    carry = carry_ref[...]
    for c in range(tm // LANES):
        hi, mid, lo = _split3(lf[:, c * LANES:(c + 1) * LANES])
        cs = _dot(hi, tri) + _dot(mid, tri) + _dot(lo, tri) + carry
        cum_ref[0, :, c * LANES:(c + 1) * LANES] = cs
        carry = jnp.broadcast_to(cs[:, LANES - 1:LANES], carry.shape)
    carry_ref[...] = carry


def _proj_prompt(h, mod, gain, wq, wkvt, wft, bf, batch, seq):
    m, d = h.shape
    tm = ROW_TILE
    tpb = seq // tm
    n_pair = D_HEADS // LANES
    t_spec = lambda rows: pl.BlockSpec((1, rows, tm), lambda i: (i // tpb, 0, i % tpb))
    t_shape = lambda rows: jax.ShapeDtypeStruct((batch, rows, seq), F32)
    b_spec = lambda n, w: pl.BlockSpec((1, n, tm // w, LANES, w), lambda i: (i // tpb, 0, i % tpb, 0, 0))
    b_shape = lambda n, w: jax.ShapeDtypeStruct((batch, n, seq // w, LANES, w), BF)
    tk = FOX_KEY_TILE
    return pl.pallas_call(
        functools.partial(_proj_prompt_kernel, tiles_per_batch=tpb),
        grid=(m // tm,),
        in_specs=[pl.BlockSpec((tm, d), lambda i: (i, 0)), _const_spec((1, d)),
                  _mod_spec(mod, tm, tpb, 3), _mod_spec(mod, tm, tpb, 4),
                  _const_spec(wq.shape), _const_spec(wkvt.shape), _const_spec(wft.shape),
                  _const_spec(bf.shape)],
        out_specs=[pl.BlockSpec((1, tm, wq.shape[1]), lambda i: (i // tpb, i % tpb, 0)),
                   t_spec(D_HEADS), t_spec(D_HEADS), t_spec(D_HEADS), t_spec(D_HEADS),
                   b_spec(2 * n_pair, ATT_TILE), b_spec(n_pair, tk), b_spec(N_HEADS, tk),
                   t_spec(N_HEADS), t_spec(N_HEADS)],
        out_shape=[jax.ShapeDtypeStruct((batch, seq, wq.shape[1]), BF),
                   t_shape(D_HEADS), t_shape(D_HEADS), t_shape(D_HEADS), t_shape(D_HEADS),
                   b_shape(2 * n_pair, ATT_TILE), b_shape(n_pair, tk), b_shape(N_HEADS, tk),
                   t_shape(N_HEADS), t_shape(N_HEADS)],
        scratch_shapes=[pltpu.VMEM((N_HEADS, LANES), F32)],
        compiler_params=_params(("arbitrary",)),
        name="proj_prompt",
    )(h, gain, mod, mod, wq, wkvt, wft, bf)


def _head_masks():
    lane = lax.broadcasted_iota(jnp.int32, (ATT_TILE, LANES), 1)
    return lane < HEAD_DIM


def _head_queries(q_ref):
    first = _head_masks()
    qs = []
    for pair in range(q_ref.shape[2] // LANES):
        q = q_ref[0, :, pair * LANES:(pair + 1) * LANES]
        zero = jnp.zeros_like(q)
        qs += [jnp.where(first, q, zero), jnp.where(first, zero, q)]
    return first, qs


def _skewed(n, stages):
    depth = len(stages)
    for step in range(n + depth - 1):
        for k in reversed(range(depth)):
            if 0 <= step - k < n:
                stages[k](step - k)


def _store_heads(o_ref, first, outs):
    for pair in range(len(outs) // 2):
        o_ref[0, :, pair * LANES:(pair + 1) * LANES] = jnp.where(first, outs[2 * pair], outs[2 * pair + 1]).astype(BF)


def _sb_prompt_kernel(q_ref, k_ref, v_ref, o_ref):
    qi = pl.program_id(2)
    t = ATT_TILE
    first, qs = _head_queries(q_ref)
    n_h = len(qs)
    tri = _tri(t, lambda r, c: r > c)
    row = lax.broadcasted_iota(jnp.int32, (t, t), 0)
    col = lax.broadcasted_iota(jnp.int32, (t, t), 1)
    causal = col < row

    heads = range(n_h)

    def blocks(kbs, carries, accs, mask):
        carries, accs = list(carries), list(accs)
        items = [(kb, h) for kb in kbs for h in heads]
        zs = [_dot(qs[h], k_ref[0, h // 2, kb]) for kb, h in items]
        sps = [jnp.maximum(z, 0.0) + jnp.log2(1.0 + jnp.exp2(-jnp.abs(z))) for z in zs]
        logsig = [z - sp for z, sp in zip(zs, sps)]
        if mask is not None:
            sps = [jnp.where(mask, sp, 0.0) for sp in sps]
        rests = [_dot(sp.astype(BF), tri) for sp in sps]
        totals = [jnp.sum(sp, axis=-1, keepdims=True) for sp in sps]
        pvs = []
        for i, (kb, h) in enumerate(items):
            a = jnp.exp2(logsig[i] - rests[i] - carries[h])
            if mask is not None:
                a = jnp.where(mask, a, 0.0)
            pvs.append(_dot_nt(a.astype(BF), v_ref[0, h // 2, kb]))
            carries[h] = carries[h] + totals[i]
        for i, (kb, h) in enumerate(items):
            accs[h] = accs[h] + pvs[i]
        return carries, accs

    carries, accs = blocks([qi], [jnp.zeros((t, 1), F32)] * n_h, [jnp.zeros((t, LANES), F32)] * n_h, causal)
    state = tuple(carries) + tuple(accs)

    def step(kbs, st):
        carries, accs = blocks(kbs, st[:n_h], st[n_h:], None)
        return tuple(carries) + tuple(accs)

    state = lax.cond(qi % 2 == 1, lambda st: step([qi - 1], st), lambda st: st, state)
    top = qi - 1 - qi % 2
    state = lax.cond(qi % 4 >= 2, lambda st: step([top, top - 1], st), lambda st: st, state)
    top = qi - 1 - qi % 4
    state = lax.fori_loop(0, qi // 4, lambda j, st: step([top - 4 * j - i for i in range(4)], st), state)
    _store_heads(o_ref, first, list(state[n_h:]))


def _fox_prompt_kernel(q_ref, k_ref, v_ref, c_ref, o_ref):
    qi = pl.program_id(2)
    t = ATT_TILE
    tk = k_ref.shape[-1]
    per = tk // t
    kd = qi // per
    first, qs = _head_queries(q_ref)
    n_h = len(qs)
    row = lax.broadcasted_iota(jnp.int32, (t, tk), 0)
    col = lax.broadcasted_iota(jnp.int32, (t, tk), 1)
    causal = col <= row + (qi % per) * t
    bases = [c_ref[0, h, pl.ds(kd, 1), :][:, 0:1] for h in range(n_h)]

    def blocks(kbs, ms, accs, mask):
        ms, accs = list(ms), list(accs)
        items = [(kb, h) for kb in kbs for h in range(n_h)]
        s, mrow, alpha, pv = {}, {}, {}, {}

        def logits(i):
            kb, h = items[i]
            s[i] = _dot(qs[h], k_ref[0, h // 2, kb]) - (c_ref[0, h, pl.ds(kb, 1), :] - bases[h]) * LOG2E
            if mask is not None:
                s[i] = jnp.where(mask, s[i], NEG_BIG)

        def row_max(i):
            h = items[i][1]
            mrow[i] = jnp.maximum(ms[h], jnp.max(s[i], axis=-1, keepdims=True))
            alpha[i] = jnp.exp2(ms[h] - mrow[i])
            ms[h] = mrow[i]

        def weights(i):
            kb, h = items[i]
            pv[i] = _dot_nt(jnp.exp2(s[i] - mrow[i]).astype(BF), v_ref[0, h, kb])

        def accumulate(i):
            h = items[i][1]
            accs[h] = alpha[i] * accs[h] + pv[i]

        _skewed(len(items), (logits, row_max, weights, accumulate))
        return tuple(ms) + tuple(accs)

    state = blocks([kd], [jnp.full((t, 1), NEG_BIG, F32)] * n_h, [jnp.zeros((t, LANES), F32)] * n_h, causal)
    state = lax.cond(kd % 2 == 1, lambda st: blocks([kd - 1], st[:n_h], st[n_h:], None), lambda st: st, state)
    first_pair = kd - 1 - kd % 2
    state = lax.fori_loop(
        0, kd // 2,
        lambda j, st: blocks([first_pair - 2 * j, first_pair - 2 * j - 1], st[:n_h], st[n_h:], None), state)
    outs = []
    for h in range(n_h):
        acc = state[n_h + h]
        denom_lane = HEAD_DIM if h % 2 == 0 else 0
        outs.append(acc / acc[:, denom_lane:denom_lane + 1])
    _store_heads(o_ref, first, outs)


def _prompt_attention(q, sbb, fxk, fxv, cum, batch, seq):
    t = ATT_TILE
    pps = PAIRS_PER_STEP
    n_pair = D_HEADS // LANES
    n_grp = n_pair // pps
    grid = (batch, n_grp, seq // t)
    sem = ("arbitrary", "arbitrary", "arbitrary")
    width = pps * LANES

    def kv_spec(blocked, first_block, per_step=pps):
        return pl.BlockSpec((1, per_step) + blocked.shape[2:], lambda b, g, qi: (b, first_block + g, 0, 0, 0),
                            pipeline_mode=pl.Buffered(1))

    def q_spec(first_block):
        return pl.BlockSpec((1, t, width), lambda b, g, qi: (b, qi, first_block + g))

    out_spec = pl.BlockSpec((1, t, width), lambda b, g, qi: (b, qi, g))
    out_shape = jax.ShapeDtypeStruct((batch, seq, D_HEADS), BF)
    o_sb = pl.pallas_call(
        _sb_prompt_kernel,
        grid=grid,
        in_specs=[q_spec(0), kv_spec(sbb, 0), kv_spec(sbb, n_grp)],
        out_specs=out_spec,
        out_shape=out_shape,
        compiler_params=_params(sem),
        name="sb_prompt",
    )(q, sbb, sbb)
    o_fx = pl.pallas_call(
        _fox_prompt_kernel,
        grid=grid,
        in_specs=[q_spec(n_grp), kv_spec(fxk, 0), kv_spec(fxv, 0, 2 * pps),
                  pl.BlockSpec((1, 2 * pps) + cum.shape[2:], lambda b, g, qi: (b, g, 0, 0))],
        out_specs=out_spec,
        out_shape=out_shape,
        compiler_params=_params(sem),
        name="fox_prompt",
    )(q, fxk, fxv, cum)
    return o_sb, o_fx


def _post_kernel(h_ref, osb_ref, ofx_ref, gmix_ref, sh2_ref, sc2_ref, gt2_ref, gffn_ref, sh3_ref, sc3_ref,
                 gt3_ref, gfin_ref, wgs_ref, wgf_ref, wbs_ref, wbf_ref, wo_ref, wup_ref, wd_ref,
                 y_ref, acc_ref):
    h = h_ref[...]
    n = _norm_mod(h, gmix_ref[...], sh2_ref[0], sc2_ref[0]).astype(BF)
    merged = (_sigmoid(_dot(n, wgs_ref[...])) * _dot(osb_ref[...], wbs_ref[...])
              + _sigmoid(_dot(n, wgf_ref[...])) * _dot(ofx_ref[...], wbf_ref[...]))
    h2 = h + gt2_ref[0] * _dot(merged.astype(BF), wo_ref[...])
    xn = _norm_mod(h2, gffn_ref[...], sh3_ref[0], sc3_ref[0]).astype(BF)
    _swiglu_acc(xn, wup_ref, wd_ref, acc_ref)
    h3 = h2 + 0.5 * gt3_ref[0] * acc_ref[...]
    y_ref[...] = _rms(h3, gfin_ref[...])


def _post(h, o_sb, o_fx, mod, gmix, gffn, gfin, wgs, wgf, wbs, wbf, wo, wup, wd, tm, tiles_per_group):
    m, d = h.shape
    row = pl.BlockSpec((tm, d), lambda i: (i, 0))
    orow = pl.BlockSpec((tm, o_sb.shape[1]), lambda i: (i, 0))
    ms = lambda j: _mod_spec(mod, tm, tiles_per_group, j)
    return pl.pallas_call(
        _post_kernel,
        grid=(m // tm,),
        in_specs=[row, orow, orow, _const_spec((1, d)), ms(3), ms(4), ms(5), _const_spec((1, d)), ms(6), ms(7),
                  ms(8), _const_spec((1, d)),
                  _const_spec(wgs.shape), _const_spec(wgf.shape), _const_spec(wbs.shape), _const_spec(wbf.shape),
                  _const_spec(wo.shape), _const_spec(wup.shape), _const_spec(wd.shape)],
        out_specs=row,
        out_shape=jax.ShapeDtypeStruct((m, d), F32),
        scratch_shapes=[pltpu.VMEM((tm, d), F32)],
        compiler_params=_params(("arbitrary",)),
        name="post",
    )(h, o_sb, o_fx, gmix, mod, mod, mod, gffn, mod, mod, mod, gfin, wgs, wgf, wbs, wbf, wo, wup, wd)


def _proj_sample_kernel(h_ref, g_ref, sh_ref, sc_ref, w_ref, wf_ref, bf_ref, p_ref, lf_ref):
    n = _norm_mod(h_ref[...], g_ref[...], sh_ref[0], sc_ref[0]).astype(BF)
    p_ref[...] = _dot(n, w_ref[...])
    lf_ref[...] = _log_sigmoid(_dot_nt(n, wf_ref[...])[:, 0:N_HEADS] + bf_ref[...])


def _proj_sample(h, mod, gain, w, wft, bf_row):
    m, d = h.shape
    return pl.pallas_call(
        _proj_sample_kernel,
        grid=(1,),
        in_specs=[_const_spec((m, d)), _const_spec((1, d)), _mod_spec(mod, m, 1, 3), _mod_spec(mod, m, 1, 4),
                  _const_spec(w.shape), _const_spec(wft.shape), _const_spec(bf_row.shape)],
        out_specs=[_const_spec((m, w.shape[1])), _const_spec((m, N_HEADS))],
        out_shape=[jax.ShapeDtypeStruct((m, w.shape[1]), F32), jax.ShapeDtypeStruct((m, N_HEADS), F32)],
        compiler_params=_params(("arbitrary",)),
        name="proj_sample",
    )(h, gain, mod, mod, w, wft, bf_row)


def _decode_kernel(pt_ref, qb_ref, *rest, fox, n_steps):
    if fox:
        (knew_ref, vnew_ref, lfnew_ref, k_hbm, v_hbm, lf_hbm, o_ref,
         kbuf, vbuf, lfbuf, zbuf, abuf, acc_ref, carry_ref, m_ref, l_ref, sem) = rest
    else:
        k_hbm, v_hbm, o_ref, kbuf, vbuf, zbuf, abuf, acc_ref, carry_ref, sem = rest
    b = pl.program_id(0)
    c = pl.program_id(1)
    step = b * n_steps + c
    total = pl.num_programs(0) * n_steps
    ch = PAGES_PER_STEP

    def copies(s, slot):
        sb = s // n_steps
        first_page = (n_steps - 1 - s % n_steps) * ch
        out = []
        for p in range(ch):
            page = pt_ref[sb, first_page + p]
            out.append(pltpu.make_async_copy(k_hbm.at[page], kbuf.at[slot, p], sem.at[0, slot]))
            out.append(pltpu.make_async_copy(v_hbm.at[page], vbuf.at[slot, p], sem.at[1, slot]))
            if fox:
                out.append(pltpu.make_async_copy(lf_hbm.at[page], lfbuf.at[slot, p], sem.at[2, slot]))
        return out

    slot = step % 2

    def start_all(cps):
        for i, cp in enumerate(cps):
            cp.start(priority=i % 2)

    @pl.when(step == 0)
    def _():
        start_all(copies(step, slot))

    @pl.when(step + 1 < total)
    def _():
        start_all(copies(step + 1, 1 - slot))

    sel_r =lax.broadcasted_iota(jnp.int32, (N_HEADS, 2 * N_HEADS * 8), 0)
    sel_c = lax.broadcasted_iota(jnp.int32, (N_HEADS, 2 * N_HEADS * 8), 1)
    sel = jnp.where((sel_c % (N_HEADS * 8)) // 8 == sel_r, 1.0, 0.0).astype(BF)

    def scores(page_ref):
        parts = []
        for h in range(N_HEADS):
            part = page_ref[h, 0:8, :] * qb_ref[0, h, 0:8, :]
            for g in range(1, HEAD_DIM // 8):
                part = part + page_ref[h, g * 8:(g + 1) * 8, :] * qb_ref[0, h, g * 8:(g + 1) * 8, :]
            parts.append(part)
        stack = jnp.concatenate(parts, axis=0)
        hi, lo = _split2(stack)
        return _dot(sel, jnp.concatenate([hi, lo], axis=0))

    @pl.when(c == 0)
    def _():
        if fox:
            m_ref[...] = scores(knew_ref.at[0]) * (QK_SCALE * LOG2E)
            l_ref[...] = jnp.ones_like(l_ref)
            carry_ref[...] = jnp.broadcast_to(lfnew_ref[0], carry_ref.shape)
            for h in range(N_HEADS):
                acc_ref[h] = jnp.where(lax.broadcasted_iota(jnp.int32, (HEAD_DIM, LANES), 1) == 0,
                                       vnew_ref[0, h], 0.0)
        else:
            carry_ref[...] = jnp.zeros_like(carry_ref)
            acc_ref[...] = jnp.zeros_like(acc_ref)

    for cp in copies(step, slot):
        cp.wait()

    for p in range(ch):
        zbuf[p] = scores(kbuf.at[slot, p])

    carry = carry_ref[...]
    tri = _tri(LANES, lambda r, cc: r > cc)
    if fox:
        tri3 = jnp.concatenate([tri, tri, tri], axis=0)
        logits = [None] * ch
        for p in reversed(range(ch)):
            lf = lfbuf[slot, p]
            hi, mid, lo = _split3(lf)
            suffix = _dot(jnp.concatenate([hi, mid, lo], axis=1), tri3)
            logits[p] = (zbuf[p] * QK_SCALE + suffix + carry) * LOG2E
            carry = carry + jnp.sum(lf, axis=-1, keepdims=True)
        mx = logits[0]
        for p in range(1, ch):
            mx = jnp.maximum(mx, logits[p])
        m_old = m_ref[...]
        m_new = jnp.maximum(m_old, jnp.max(mx, axis=-1, keepdims=True))
        alpha = jnp.exp2(m_old - m_new)
        lsum = jnp.zeros((N_HEADS, LANES), F32)
        for p in range(ch):
            a = jnp.exp2(logits[p] - m_new)
            abuf[p] = a
            lsum = lsum + a
        l_ref[...] = alpha * l_ref[...] + jnp.sum(lsum, axis=-1, keepdims=True)
        m_ref[...] = m_new
    else:
        tri2 = jnp.concatenate([tri, tri], axis=0)
        for p in reversed(range(ch)):
            z = zbuf[p] * (QK_SCALE * LOG2E)
            sp = jnp.maximum(z, 0.0) + jnp.log2(1.0 + jnp.exp2(-jnp.abs(z)))
            hi, lo = _split2(sp)
            rest_ = _dot(jnp.concatenate([hi, lo], axis=1), tri2)
            abuf[p] = jnp.exp2(z - sp - rest_ - carry)
            carry = carry + jnp.sum(sp, axis=-1, keepdims=True)
    carry_ref[...] = carry

    for h in range(N_HEADS):
        acc = acc_ref[h]
        if fox:
            acc = acc * alpha[h:h + 1, :]
        for p in range(ch):
            acc = acc + abuf[p, h:h + 1, :] * vbuf[slot, p, h]
        acc_ref[h] = acc

    @pl.when(c == n_steps - 1)
    def _():
        for h in range(N_HEADS):
            acc = acc_ref[h]
            if fox:
                acc = acc / l_ref[h:h + 1, :]
            o_ref[0, h] = jnp.sum(acc, axis=-1, keepdims=True)


def _decode(page_table, qb, k_cache, v_cache, fox_args=None):
    n, n_pages = page_table.shape
    fox = fox_args is not None
    ch = PAGES_PER_STEP
    n_steps = n_pages // ch
    page = k_cache.shape[1:]
    seq_spec = pl.BlockSpec((1,) + qb.shape[1:], lambda b, c, pt: (b, 0, 0, 0))
    any_spec = pl.BlockSpec(memory_space=pl.ANY)
    in_specs = [seq_spec]
    args = [qb]
    scratch = [pltpu.VMEM((2, ch) + page, F32), pltpu.VMEM((2, ch) + page, F32)]
    if fox:
        knew, vnew, lfnew, lf_cache = fox_args
        in_specs += [seq_spec, seq_spec, pl.BlockSpec((1, N_HEADS, 1), lambda b, c, pt: (b, 0, 0))]
        args += [knew, vnew, lfnew]
        scratch.append(pltpu.VMEM((2, ch, N_HEADS, LANES), F32))
    in_specs += [any_spec, any_spec]
    args += [k_cache, v_cache]
    if fox:
        in_specs.append(any_spec)
        args.append(lf_cache)
    scratch += [pltpu.VMEM((ch, N_HEADS, LANES), F32), pltpu.VMEM((ch, N_HEADS, LANES), F32),
                pltpu.VMEM((N_HEADS, HEAD_DIM, LANES), F32), pltpu.VMEM((N_HEADS, LANES), F32)]
    if fox:
        scratch += [pltpu.VMEM((N_HEADS, LANES), F32), pltpu.VMEM((N_HEADS, LANES), F32)]
    scratch.append(pltpu.SemaphoreType.DMA((3, 2)))
    return pl.pallas_call(
        functools.partial(_decode_kernel, fox=fox, n_steps=n_steps),
        grid_spec=pltpu.PrefetchScalarGridSpec(
            num_scalar_prefetch=1,
            grid=(n, n_steps),
            in_specs=in_specs,
            out_specs=pl.BlockSpec((1, N_HEADS, HEAD_DIM, 1), lambda b, c, pt: (b, 0, 0, 0)),
            scratch_shapes=scratch),
        out_shape=jax.ShapeDtypeStruct((n, N_HEADS, HEAD_DIM, 1), F32),
        compiler_params=_params(("arbitrary", "arbitrary")),
        name="fox_decode" if fox else "sb_decode",
    )(page_table, *args)


def _ffn_weights(w_up, w_down):
    assert w_down.shape[0] % FF_CHUNK == 0
    return w_up.astype(BF), w_down.astype(BF)


def _lane_rep(x):
    n = x.shape[0]
    return jnp.broadcast_to(x.reshape(n, N_HEADS, HEAD_DIM, 1), (n, N_HEADS, HEAD_DIM, LANES))


def kernel(x_prompt, x_sample, c_prompt, c_sample, cache_sb_k, cache_sb_v, cache_fox_k, cache_fox_v, cache_fox_logf, page_table, norm_ffn1, norm_mix, norm_ffn2, w_ada, b_ada, w_up1, w_down1, w_up2, w_down2, w_in, b_forget, w_br_sb, w_br_fox, w_out, norm_final):
    batch, seq, d = x_prompt.shape
    n_dec = x_sample.shape[0]
    depth = w_ada.shape[0]
    assert depth == 1 and x_sample.shape[1] == 1
    assert seq % ROW_TILE == 0 and ROW_TILE % ATT_TILE == 0
    assert page_table.shape[1] % PAGES_PER_STEP == 0 and cache_sb_k.shape[2] == LANES

    wt = jnp.transpose(w_in[0])
    dh = D_HEADS
    wq = jnp.concatenate([wt[0:dh], wt[3 * dh:4 * dh]], axis=0).T.astype(BF)
    wkvt = jnp.concatenate([wt[dh:3 * dh], wt[4 * dh:6 * dh]], axis=0).astype(BF)
    wqkv_s = wt[0:6 * dh].T.astype(BF)
    wft = jnp.concatenate([wt[6 * dh:6 * dh + N_HEADS], jnp.zeros((N_HEADS, d), F32)], axis=0).astype(BF)
    wgs = wt[6 * dh + N_HEADS:6 * dh + N_HEADS + d].T.astype(BF)
    wgf = wt[6 * dh + N_HEADS + d:].T.astype(BF)
    wbs, wbf, wo = w_br_sb[0].astype(BF), w_br_fox[0].astype(BF), w_out[0].astype(BF)
    ffn1_w = _ffn_weights(w_up1[0], w_down1[0])
    ffn2_w = _ffn_weights(w_up2[0], w_down2[0])
    g1, gm, g2, gf = norm_ffn1, norm_mix, norm_ffn2, norm_final.reshape(1, d)
    bf_col = b_forget.reshape(N_HEADS, 1)
    bf_row = b_forget.reshape(1, N_HEADS)

    n_c = batch + n_dec
    n_pad = -n_c % 8
    c_all = jnp.concatenate([c_prompt, c_sample, jnp.zeros((n_pad, d), F32)], axis=0)
    mod = _adaln(c_all, w_ada[0], b_ada)
    mod_p = mod[:batch].reshape(batch, 1, N_MOD * d)
    mod_s = mod[batch:n_c].reshape(1, n_dec, N_MOD * d)

    tpb = seq // ROW_TILE
    xp = x_prompt.reshape(batch * seq, d)
    h = _ffn(xp, mod_p, g1, *ffn1_w, ROW_TILE, tpb, 0)
    q, ksb_t, vsb_t, kfx_t, vfx_t, sbb, fxk, fxv, lft, cum = _proj_prompt(h, mod_p, gm, wq, wkvt, wft, bf_col,
                                                                          batch, seq)
    cum = cum.reshape(batch, N_HEADS, seq // FOX_KEY_TILE, FOX_KEY_TILE)
    o_sb, o_fx = _prompt_attention(q, sbb, fxk, fxv, cum, batch, seq)
    y_prompt = _post(h, o_sb.reshape(batch * seq, dh), o_fx.reshape(batch * seq, dh), mod_p, gm, g2, gf,
                     wgs, wgf, wbs, wbf, wo, *ffn2_w, ROW_TILE, tpb).reshape(batch, seq, d)

    def state(t):
        return jnp.transpose(t.reshape(batch, N_HEADS, HEAD_DIM, seq), (0, 3, 1, 2))[None]

    lf_prompt = jnp.transpose(lft, (0, 2, 1))[None]

    xs = x_sample.reshape(n_dec, d)
    hs = _ffn(xs, mod_s, g1, *ffn1_w, n_dec, 1, 0)
    ps, lfs = _proj_sample(hs, mod_s, gm, wqkv_s, wft, bf_row)
    q_sb, k_sb, v_sb, q_fx, k_fx, v_fx = [ps[:, i * dh:(i + 1) * dh] for i in range(6)]
    view = lambda cache: jnp.transpose(cache[0], (0, 2, 3, 1))
    o_sb_s = _decode(page_table, _lane_rep(q_sb), view(cache_sb_k), view(cache_sb_v))
    lf_view = jnp.transpose(cache_fox_logf[0], (0, 2, 1))
    o_fx_s = _decode(page_table, _lane_rep(q_fx), view(cache_fox_k), view(cache_fox_v),
                     (_lane_rep(k_fx), _lane_rep(v_fx), lfs.reshape(n_dec, N_HEADS, 1), lf_view))
    y_sample = _post(hs, o_sb_s.reshape(n_dec, dh).astype(BF), o_fx_s.reshape(n_dec, dh).astype(BF), mod_s,
                     gm, g2, gf, wgs, wgf, wbs, wbf, wo, *ffn2_w, n_dec, 1).reshape(n_dec, 1, d)

    st = lambda x: x.reshape(1, n_dec, 1, N_HEADS, HEAD_DIM)
    return (y_prompt, y_sample, state(ksb_t), state(vsb_t), state(kfx_t), state(vfx_t), lf_prompt,
            st(k_sb), st(v_sb), st(k_fx), st(v_fx), lfs.reshape(1, n_dec, 1, N_HEADS))
```
